```python
import jax, jax.numpy as jnp
from jax import lax
import numpy as np

D_MODEL = 1024
BATCH = 8
SEQ = 2048
DEPTH = 4

HEAD_DIM = 64
SB_HEADS = 8
SB_WIDTH = SB_HEADS * HEAD_DIM
POOL_WINDOWS = (2, 4, 8, 16)
POOL_WIDTH = D_MODEL // 2
POOL_GROUP = POOL_WIDTH // len(POOL_WINDOWS)
EVEN_IN = 3 * SB_WIDTH + POOL_WIDTH
EVEN_MIX = SB_WIDTH + POOL_WIDTH
MOBA_HEADS = D_MODEL // HEAD_DIM
MOBA_WIDTH = MOBA_HEADS * HEAD_DIM
MOBA_BLOCK = 256
MOBA_TOPK = 3
MOBA_QCHUNK = 16
Q_BLOCK = 128
D_FF = 2816
N_EVEN = (DEPTH + 1) // 2
N_ODD = DEPTH // 2
RMS_EPS = 1e-6

kernel_name = "hybrid_stickbreak_pool_moba_macaron"


def rmsnorm(x, g):
    xf = x.astype(jnp.float32)
    y = xf * lax.rsqrt(jnp.mean(xf * xf, axis=-1, keepdims=True) + RMS_EPS)
    return (y * g.astype(jnp.float32)).astype(x.dtype)


def swiglu(h, w_gate, w_up, w_down):
    return (jax.nn.silu(h @ w_gate) * (h @ w_up)) @ w_down


def split_heads(t, n):
    b, s, _ = t.shape
    return t.reshape(b, s, n, HEAD_DIM).transpose(0, 2, 1, 3)


def merge_heads(t):
    b, h, s, dh = t.shape
    return t.transpose(0, 2, 1, 3).reshape(b, s, h * dh)


def alibi_slopes(n):
    return jnp.asarray(2.0 ** (-8.0 * np.arange(1, n + 1) / n), dtype=jnp.float32)


def stick_breaking_attention(q, k, v):
    s_len = q.shape[2]
    scale = HEAD_DIM ** -0.5
    outs = []
    for i in range(s_len // Q_BLOCK):
        q0 = i * Q_BLOCK
        kv_len = q0 + Q_BLOCK
        qb = q[:, :, q0:kv_len]
        kb = k[:, :, :kv_len]
        vb = v[:, :, :kv_len]
        z = jnp.einsum('bhtd,bhsd->bhts', qb, kb).astype(jnp.float32) * scale
        tpos = q0 + jnp.arange(Q_BLOCK)[:, None]
        spos = jnp.arange(kv_len)[None, :]
        past = spos < tpos
        log_1m = jnp.where(past, jax.nn.log_sigmoid(-z), 0.0)
        later = lax.cumsum(log_1m, axis=3, reverse=True) - log_1m
        a = jnp.where(past, jnp.exp(jax.nn.log_sigmoid(z) + later), 0.0)
        outs.append(jnp.einsum('bhts,bhsd->bhtd', a.astype(v.dtype), vb))
    return jnp.concatenate(outs, axis=2)


def multiscale_pool(u, w_pool, pool_scale):
    b, s, _ = u.shape
    uf = u.astype(jnp.float32)
    cs = jnp.concatenate([jnp.zeros((b, 1, POOL_WIDTH), jnp.float32), jnp.cumsum(uf, axis=1)], axis=1)
    t = jnp.arange(s)
    groups = []
    for g, w in enumerate(POOL_WINDOWS):
        sl = slice(g * POOL_GROUP, (g + 1) * POOL_GROUP)
        start = jnp.maximum(t + 1 - w, 0)
        win_sum = cs[:, 1:, sl] - cs[:, start, sl]
        count = jnp.minimum(t + 1, w).astype(jnp.float32)[None, :, None]
        groups.append(win_sum / count - uf[:, :, sl])
    pooled = jnp.stack(groups, axis=2).astype(u.dtype)
    mixed = jnp.einsum('bsgc,gcd->bsgd', pooled, w_pool).reshape(b, s, POOL_WIDTH)
    return mixed * pool_scale


def moba_attention(q, k, v):
    b, h, s, dh = q.shape
    nb = -(-s // MOBA_BLOCK)
    pad = nb * MOBA_BLOCK - s
    kp = jnp.pad(k, ((0, 0), (0, 0), (0, pad), (0, 0)))
    vp = jnp.pad(v, ((0, 0), (0, 0), (0, pad), (0, 0)))
    k_blocks = kp.reshape(b, h, nb, MOBA_BLOCK, dh)
    v_blocks = vp.reshape(b, h, nb, MOBA_BLOCK, dh)
    k_mean = jnp.mean(k_blocks.astype(jnp.float32), axis=3).astype(k.dtype)
    gate = jnp.einsum('bhtd,bhnd->bhtn', q, k_mean).astype(jnp.float32)
    qblk_all = jnp.arange(s) // MOBA_BLOCK
    past_blk = jnp.arange(nb)[None, :] < qblk_all[:, None]
    gate = jnp.where(past_blk, gate, -jnp.inf)
    topk = min(MOBA_TOPK, nb)
    _, sel = lax.top_k(gate, topk)
    scale = dh ** -0.5
    slopes = alibi_slopes(h)
    bi = jnp.arange(b)[:, None, None, None]
    hi = jnp.arange(h)[None, :, None, None]

    def chunk(c):
        t0 = c * MOBA_QCHUNK
        qc = lax.dynamic_slice_in_dim(q, t0, MOBA_QCHUNK, axis=2)
        sel_c = lax.dynamic_slice_in_dim(sel, t0, MOBA_QCHUNK, axis=2)
        tq = t0 + jnp.arange(MOBA_QCHUNK)
        blk0 = (t0 // MOBA_BLOCK) * MOBA_BLOCK
        k_own = lax.dynamic_slice_in_dim(kp, blk0, MOBA_BLOCK, axis=2)
        v_own = lax.dynamic_slice_in_dim(vp, blk0, MOBA_BLOCK, axis=2)
        own_pos = blk0 + jnp.arange(MOBA_BLOCK)
        k_sel = k_blocks[bi, hi, sel_c]
        v_sel = v_blocks[bi, hi, sel_c]
        sel_pos = sel_c[..., None] * MOBA_BLOCK + jnp.arange(MOBA_BLOCK)
        sel_ok = (sel_c < (tq // MOBA_BLOCK)[:, None])[..., None]
        d_own = (tq[:, None] - own_pos[None, :]).astype(jnp.float32)
        s_own = (jnp.einsum('bhtd,bhsd->bhts', qc, k_own).astype(jnp.float32) * scale
                 - slopes[None, :, None, None] * d_own)
        s_own = jnp.where(own_pos[None, :] <= tq[:, None], s_own, -jnp.inf)
        d_sel = (tq[:, None, None] - sel_pos).astype(jnp.float32)
        s_sel = (jnp.einsum('bhtd,bhtksd->bhtks', qc, k_sel).astype(jnp.float32) * scale
                 - slopes[None, :, None, None, None] * d_sel)
        s_sel = jnp.where(sel_ok, s_sel, -jnp.inf).reshape(b, h, MOBA_QCHUNK, topk * MOBA_BLOCK)
        p = jax.nn.softmax(jnp.concatenate([s_own, s_sel], axis=-1), axis=-1)
        p_own = p[..., :MOBA_BLOCK].astype(v.dtype)
        p_sel = p[..., MOBA_BLOCK:].reshape(b, h, MOBA_QCHUNK, topk, MOBA_BLOCK).astype(v.dtype)
        return (jnp.einsum('bhts,bhsd->bhtd', p_own, v_own)
                + jnp.einsum('bhtks,bhtksd->bhtd', p_sel, v_sel))

    outs = lax.map(chunk, jnp.arange(s // MOBA_QCHUNK))
    return outs.transpose(1, 2, 0, 3, 4).reshape(b, h, s, dh)


def even_mixer(h, w_in, w_pool, pool_scale, w_out):
    proj = h @ w_in
    q, k, v, u = jnp.split(proj, [SB_WIDTH, 2 * SB_WIDTH, 3 * SB_WIDTH], axis=-1)
    a = stick_breaking_attention(split_heads(q, SB_HEADS), split_heads(k, SB_HEADS), split_heads(v, SB_HEADS))
    p = multiscale_pool(u, w_pool, pool_scale)
    return jnp.concatenate([merge_heads(a), p], axis=-1) @ w_out


def odd_mixer(h, w_qkv, w_o):
    q, k, v = jnp.split(h @ w_qkv, 3, axis=-1)
    o = moba_attention(split_heads(q, MOBA_HEADS), split_heads(k, MOBA_HEADS), split_heads(v, MOBA_HEADS))
    return merge_heads(o) @ w_o


def setup_inputs(seed: int = 0) -> dict:
    key = jax.random.key(seed)
    ks = jax.random.split(key, 20)
    f32 = jnp.float32

    def w(k, shape, fan_in):
        return jax.random.normal(k, shape, f32) * (fan_in ** -0.5)

    def gain(k, shape):
        return 1.0 + 0.05 * jax.random.normal(k, shape, f32)

    return {
        "x": jax.random.normal(ks[0], (BATCH, SEQ, D_MODEL), f32),
        "norm_ffn1": gain(ks[1], (DEPTH, D_MODEL)),
        "ffn1_gate": w(ks[2], (DEPTH, D_MODEL, D_FF), D_MODEL),
        "ffn1_up": w(ks[3], (DEPTH, D_MODEL, D_FF), D_MODEL),
        "ffn1_down": w(ks[4], (DEPTH, D_FF, D_MODEL), D_FF),
        "norm_mix": gain(ks[5], (DEPTH, D_MODEL)),
        "norm_ffn2": gain(ks[6], (DEPTH, D_MODEL)),
        "ffn2_gate": w(ks[7], (DEPTH, D_MODEL, D_FF), D_MODEL),
        "ffn2_up": w(ks[8], (DEPTH, D_MODEL, D_FF), D_MODEL),
        "ffn2_down": w(ks[9], (DEPTH, D_FF, D_MODEL), D_FF),
        "even_w_in": w(ks[10], (N_EVEN, D_MODEL, EVEN_IN), D_MODEL),
        "even_w_pool": w(ks[11], (N_EVEN, len(POOL_WINDOWS), POOL_GROUP, POOL_GROUP), POOL_GROUP),
        "even_pool_scale": gain(ks[12], (N_EVEN, POOL_WIDTH)),
        "even_w_out": w(ks[13], (N_EVEN, EVEN_MIX, D_MODEL), EVEN_MIX),
        "odd_w_qkv": w(ks[14], (N_ODD, D_MODEL, 3 * MOBA_WIDTH), D_MODEL),
        "odd_w_o": w(ks[15], (N_ODD, MOBA_WIDTH, D_MODEL), MOBA_WIDTH),
        "norm_final": gain(ks[16], (D_MODEL,)),
    }


def reference(x, norm_ffn1, ffn1_gate, ffn1_up, ffn1_down, norm_mix, norm_ffn2,
              ffn2_gate, ffn2_up, ffn2_down, even_w_in, even_w_pool, even_pool_scale,
              even_w_out, odd_w_qkv, odd_w_o, norm_final):
    for layer in range(DEPTH):
        x = x + 0.5 * swiglu(rmsnorm(x, norm_ffn1[layer]), ffn1_gate[layer], ffn1_up[layer], ffn1_down[layer])
        h = rmsnorm(x, norm_mix[layer])
        i = layer // 2
        if layer % 2 == 0:
            x = x + even_mixer(h, even_w_in[i], even_w_pool[i], even_pool_scale[i], even_w_out[i])
        else:
            x = x + odd_mixer(h, odd_w_qkv[i], odd_w_o[i])
        x = x + 0.5 * swiglu(rmsnorm(x, norm_ffn2[layer]), ffn2_gate[layer], ffn2_up[layer], ffn2_down[layer])
    return rmsnorm(x, norm_final)
```

```python
import functools

import jax
import jax.numpy as jnp
import numpy as np
from jax import lax
from jax.experimental import pallas as pl
from jax.experimental.pallas import tpu as pltpu

F32 = jnp.float32
BF16 = jnp.bfloat16

LANES = 128
HEAD_DIM = 64
HEADS_PER_BLOCK = LANES // HEAD_DIM
ATT_TILE = 256
MOBA_BLOCK = 256
MOBA_TOPK = 3
POOL_WINDOWS = (2, 4, 8, 16)
POOL_HALO = 16
RMS_EPS = 1e-6
MASKED = -1e30
TOKEN_TILE = 512
FF_CHUNK = 1024
VMEM_LIMIT = 56 * 1024 * 1024


def _rmsnorm(x, g):
    return x * lax.rsqrt(jnp.mean(x * x, axis=-1, keepdims=True) + RMS_EPS) * g


def _resident(shape):
    nd = len(shape)
    return pl.BlockSpec(shape, lambda *_: (0,) * nd, pipeline_mode=pl.Buffered(1))


def _dot(a, b):
    return jnp.dot(a, b, preferred_element_type=F32)


def _dot_nt(a, b):
    return lax.dot_general(a, b, (((1,), (1,)), ((), ())), preferred_element_type=F32)


def _split_bf16(x):
    hi = x.astype(BF16)
    lo = (x - hi.astype(F32)).astype(BF16)
    return hi, lo


def _ffn_body(x_ref, g_ref, wg_ref, wu_ref, wd_ref, gf_ref, o_ref, *, chunks, final):
    x = x_ref[...]
    h = _rmsnorm(x, g_ref[...]).astype(BF16)
    acc = None
    for c0, c1 in chunks:
        gate = _dot(h, wg_ref[:, c0:c1])
        up = _dot(h, wu_ref[:, c0:c1])
        act = (gate / (1.0 + jnp.exp(-gate)) * up).astype(BF16)
        down = _dot(act, wd_ref[c0:c1, :])
        acc = down if acc is None else acc + down
    y = x + 0.5 * acc
    if final:
        y = _rmsnorm(y, gf_ref[...])
    o_ref[...] = y


def _ffn(x2, g, wg, wu, wd, gf, final):
    m, d = x2.shape
    ff = wg.shape[1]
    chunks = tuple((c, min(c + FF_CHUNK, ff)) for c in range(0, ff, FF_CHUNK))
    tm = TOKEN_TILE
    return pl.pallas_call(
        functools.partial(_ffn_body, chunks=chunks, final=final),
        grid=(m // tm,),
        in_specs=[
            pl.BlockSpec((tm, d), lambda i: (i, 0)),
            _resident((1, d)),
            _resident((d, ff)),
            _resident((d, ff)),
            _resident((ff, d)),
            _resident((1, d)),
        ],
        out_specs=pl.BlockSpec((tm, d), lambda i: (i, 0)),
        out_shape=jax.ShapeDtypeStruct((m, d), F32),
        compiler_params=pltpu.CompilerParams(
            dimension_semantics=("arbitrary",), vmem_limit_bytes=VMEM_LIMIT),
        name="ffn",
    )(x2, g, wg, wu, wd, gf)


def _proj_body(x_ref, g_ref, w_ref, *o_refs, n_bf16):
    h = _rmsnorm(x_ref[...], g_ref[...]).astype(BF16)
    r = _dot(h, w_ref[...])
    o_refs[0][...] = r[:, :n_bf16].astype(BF16)
    if len(o_refs) > 1:
        o_refs[1][...] = r[:, n_bf16:]


def _proj(x2, g, w, n_bf16):
    m, d = x2.shape
    n = w.shape[1]
    tm = TOKEN_TILE
    out_specs = [pl.BlockSpec((tm, n_bf16), lambda i: (i, 0))]
    out_shape = [jax.ShapeDtypeStruct((m, n_bf16), BF16)]
    if n > n_bf16:
        out_specs.append(pl.BlockSpec((tm, n - n_bf16), lambda i: (i, 0)))
        out_shape.append(jax.ShapeDtypeStruct((m, n - n_bf16), F32))
    return pl.pallas_call(
        functools.partial(_proj_body, n_bf16=n_bf16),
        grid=(m // tm,),
        in_specs=[pl.BlockSpec((tm, d), lambda i: (i, 0)), _resident((1, d)), _resident((d, n))],
        out_specs=out_specs,
        out_shape=out_shape,
        compiler_params=pltpu.CompilerParams(
            dimension_semantics=("arbitrary",), vmem_limit_bytes=VMEM_LIMIT),
        name="proj",
    )(x2, g, w)


def _head_mask(hh):
    lane = lax.broadcasted_iota(jnp.int32, (1, LANES), 1)
    return (lane // HEAD_DIM) == hh


def _load_q(q_ref, r0, hmask):
    q = q_ref[0, pl.ds(r0, ATT_TILE), :]
    return jnp.where(hmask, q, jnp.zeros_like(q)) * jnp.asarray(HEAD_DIM ** -0.5, BF16)


def _stage_v_transposed(v_ref, vt_ref, nb):
    for j in range(nb):
        v = v_ref[0, j * ATT_TILE:(j + 1) * ATT_TILE, :].astype(F32)
        vt_ref[j] = v.T.astype(BF16)


def _emit_output(ot_ref, o_ref, nb):
    for j in range(nb):
        o_ref[0, j * ATT_TILE:(j + 1) * ATT_TILE, :] = ot_ref[j].T.astype(o_ref.dtype)


def _attn_specs(seq, qcol, kcol, vcol):
    blk = (1, seq, LANES)
    return [
        pl.BlockSpec(blk, lambda b, p: (b, 0, qcol + p)),
        pl.BlockSpec(blk, lambda b, p: (b, 0, kcol + p)),
        pl.BlockSpec(blk, lambda b, p: (b, 0, vcol + p)),
    ]


def _sb_body(q_ref, k_ref, v_ref, o_ref, vt_ref, ot_ref, *, nb):
    t = ATT_TILE
    _stage_v_transposed(v_ref, vt_ref, nb)
    s_r = lax.broadcasted_iota(jnp.int32, (t, t), 0)
    t_r = lax.broadcasted_iota(jnp.int32, (t, t), 1)
    past = s_r < t_r
    tri = (t_r > s_r).astype(BF16)

    def log_one_minus_beta(z):
        return -(jnp.maximum(z, 0.0) + jnp.log(1.0 + jnp.exp(-jnp.abs(z))))

    def later_sum(lg):
        hi, lo = _split_bf16(lg)
        c2 = _dot(tri, jnp.concatenate([hi, lo], axis=1))
        return c2[:, :t] + c2[:, t:]

    for hh in range(HEADS_PER_BLOCK):
        hmask = _head_mask(hh)
        rows = slice(hh * HEAD_DIM, (hh + 1) * HEAD_DIM)

        def q_block(qb, carry, hmask=hmask, rows=rows):
            r0 = pl.multiple_of(qb * t, t)
            qh = _load_q(q_ref, r0, hmask)

            z = _dot_nt(k_ref[0, pl.ds(r0, t), :], qh)
            lg = jnp.where(past, log_one_minus_beta(z), 0.0)
            later = later_sum(lg)
            a = jnp.where(past, jnp.exp(z + lg + later), 0.0)
            acc = _dot(vt_ref[qb, rows, :], a.astype(BF16))
            run = later[0:1, :] + lg[0:1, :]

            def k_block(jj, c):
                run, acc = c
                j = qb - 1 - jj
                c0 = pl.multiple_of(j * t, t)
                z = _dot_nt(k_ref[0, pl.ds(c0, t), :], qh)
                lg = log_one_minus_beta(z)
                later = later_sum(lg)
                a = jnp.exp(z + lg + later + run)
                acc = acc + _dot(vt_ref[j, rows, :], a.astype(BF16))
                return run + later[0:1, :] + lg[0:1, :], acc

            _, acc = lax.fori_loop(0, qb, k_block, (run, acc))
            ot_ref[qb, rows, :] = acc
            return carry

        lax.fori_loop(0, nb, q_block, 0)
    _emit_output(ot_ref, o_ref, nb)


def _sb_attention(qkv, n_heads):
    b, s, _ = qkv.shape
    nblk = n_heads // HEADS_PER_BLOCK
    nb = s // ATT_TILE
    return pl.pallas_call(
        functools.partial(_sb_body, nb=nb),
        grid=(b, nblk),
        in_specs=_attn_specs(s, 0, nblk, 2 * nblk),
        out_specs=pl.BlockSpec((1, s, LANES), lambda b_, p: (b_, 0, p)),
        out_shape=jax.ShapeDtypeStruct((b, s, nblk * LANES), BF16),
        scratch_shapes=[
            pltpu.VMEM((nb, LANES, ATT_TILE), BF16),
            pltpu.VMEM((nb, LANES, ATT_TILE), F32),
        ],
        compiler_params=pltpu.CompilerParams(
            dimension_semantics=("arbitrary", "arbitrary"), vmem_limit_bytes=VMEM_LIMIT),
        name="sb_attn",
    )(qkv, qkv, qkv)


def _moba_body(slopes_ref, q_ref, k_ref, v_ref, o_ref, vt_ref, ot_ref, ct_ref, *, nb):
    t = ATT_TILE
    pair = pl.program_id(1)
    _stage_v_transposed(v_ref, vt_ref, nb)
    s_r = lax.broadcasted_iota(jnp.int32, (t, t), 0)
    t_r = lax.broadcasted_iota(jnp.int32, (t, t), 1)
    dist = (t_r - s_r).astype(F32)
    causal = s_r <= t_r
    blk = lax.broadcasted_iota(jnp.int32, (nb, t), 0)

    kmean = jnp.concatenate(
        [jnp.sum(k_ref[0, j * t:(j + 1) * t, :].astype(F32), axis=0, keepdims=True)
         for j in range(nb)], axis=0) * (1.0 / t)

    for hh in range(HEADS_PER_BLOCK):
        hmask = _head_mask(hh)
        rows = slice(hh * HEAD_DIM, (hh + 1) * HEAD_DIM)
        slope = slopes_ref[pair * HEADS_PER_BLOCK + hh]
        bias = dist * (-slope)
        bias_diag = jnp.where(causal, bias, MASKED)
        km_hi, km_lo = _split_bf16(jnp.where(hmask, kmean, 0.0))

        def q_block(qb, carry, hmask=hmask, rows=rows, slope=slope, bias=bias,
                    bias_diag=bias_diag, km_hi=km_hi, km_lo=km_lo):
            r0 = pl.multiple_of(qb * t, t)
            qh = _load_q(q_ref, r0, hmask)

            gate = _dot_nt(km_hi, qh) + _dot_nt(km_lo, qh)
            rank = jnp.zeros((nb, t), jnp.int32)
            for n in range(nb):
                other = gate[n:n + 1, :]
                beats = (other > gate) | ((other == gate) & (n < blk))
                rank = rank + jnp.where(beats & (n < qb), 1, 0)
            chosen = (blk < qb) & (rank < MOBA_TOPK)
            ct_ref[...] = (jnp.where(chosen, 0.0, MASKED)
                           - slope * ((qb - blk) * t).astype(F32))

            st = _dot_nt(k_ref[0, pl.ds(r0, t), :], qh) + bias_diag
            m = jnp.max(st, axis=0, keepdims=True)
            p = jnp.exp(st - m)
            l = jnp.sum(p, axis=0, keepdims=True)
            acc = _dot(vt_ref[qb, rows, :], p.astype(BF16))

            def k_block(j, c):
                m, l, acc = c
                c0 = pl.multiple_of(j * t, t)
                st = _dot_nt(k_ref[0, pl.ds(c0, t), :], qh) + bias
                cj = ct_ref[pl.ds(j, 1), :]
                m_new = jnp.maximum(m, jnp.max(st, axis=0, keepdims=True) + cj)
                alpha = jnp.exp(m - m_new)
                p = jnp.exp(st + (cj - m_new))
                l = l * alpha + jnp.sum(p, axis=0, keepdims=True)
                acc = acc * alpha + _dot(vt_ref[j, rows, :], p.astype(BF16))
                return m_new, l, acc

            _, l, acc = lax.fori_loop(0, qb, k_block, (m, l, acc))
            ot_ref[qb, rows, :] = acc / l
            return carry

        lax.fori_loop(0, nb, q_block, 0)
    _emit_output(ot_ref, o_ref, nb)


def _moba_attention(qkv, n_heads):
    b, s, _ = qkv.shape
    nblk = n_heads // HEADS_PER_BLOCK
    nb = s // ATT_TILE
    slopes = jnp.asarray(2.0 ** (-8.0 * np.arange(1, n_heads + 1) / n_heads), dtype=F32)
    return pl.pallas_call(
        functools.partial(_moba_body, nb=nb),
        grid=(b, nblk),
        in_specs=[pl.BlockSpec(memory_space=pltpu.SMEM)] + _attn_specs(s, 0, nblk, 2 * nblk),
        out_specs=pl.BlockSpec((1, s, LANES), lambda b_, p: (b_, 0, p)),
        out_shape=jax.ShapeDtypeStruct((b, s, nblk * LANES), BF16),
        scratch_shapes=[
            pltpu.VMEM((nb, LANES, ATT_TILE), BF16),
            pltpu.VMEM((nb, LANES, ATT_TILE), F32),
            pltpu.VMEM((nb, ATT_TILE), F32),
        ],
        compiler_params=pltpu.CompilerParams(
            dimension_semantics=("arbitrary", "arbitrary"), vmem_limit_bytes=VMEM_LIMIT),
        name="moba_attn",
    )(slopes, qkv, qkv, qkv)


def _even_out_body(a_ref, u_ref, uprev_ref, wpool_ref, pscale_ref, wout_ref, x_ref, o_ref,
                   ucat_ref, *, tm, n_attn):
    ti = pl.program_id(1)
    halo = uprev_ref[0]
    ucat_ref[0:POOL_HALO, :] = jnp.where(ti > 0, halo, jnp.zeros_like(halo))
    ucat_ref[POOL_HALO:, :] = u_ref[0]
    pos = ti * tm + lax.broadcasted_iota(jnp.int32, (tm, 1), 0)
    mixed = []
    for g, w in enumerate(POOL_WINDOWS):
        lanes = slice(g * LANES, (g + 1) * LANES)
        win = ucat_ref[POOL_HALO:POOL_HALO + tm, lanes]
        for i in range(1, w):
            win = win + ucat_ref[POOL_HALO - i:POOL_HALO - i + tm, lanes]
        count = jnp.minimum(pos + 1, w).astype(F32)
        pooled = win / count - ucat_ref[POOL_HALO:POOL_HALO + tm, lanes]
        mixed.append(_dot(pooled.astype(BF16), wpool_ref[g]) * pscale_ref[:, lanes])
    p = jnp.concatenate(mixed, axis=1).astype(BF16)
    y = _dot(a_ref[0], wout_ref[:n_attn, :]) + _dot(p, wout_ref[n_attn:, :])
    o_ref[0] = x_ref[0] + y


def _even_out(attn, u, wpool, pscale, wout, x):
    b, s, d = x.shape
    n_attn = attn.shape[2]
    n_pool = u.shape[2]
    tm = TOKEN_TILE
    halo_blocks = tm // POOL_HALO
    return pl.pallas_call(
        functools.partial(_even_out_body, tm=tm, n_attn=n_attn),
        grid=(b, s // tm),
        in_specs=[
            pl.BlockSpec((1, tm, n_attn), lambda b_, i: (b_, i, 0)),
            pl.BlockSpec((1, tm, n_pool), lambda b_, i: (b_, i, 0)),
            pl.BlockSpec((1, POOL_HALO, n_pool),
                         lambda b_, i: (b_, jnp.maximum(i * halo_blocks - 1, 0), 0)),
            _resident(wpool.shape),
            _resident(pscale.shape),
            _resident(wout.shape),
            pl.BlockSpec((1, tm, d), lambda b_, i: (b_, i, 0)),
        ],
        out_specs=pl.BlockSpec((1, tm, d), lambda b_, i: (b_, i, 0)),
        out_shape=jax.ShapeDtypeStruct((b, s, d), F32),
        scratch_shapes=[pltpu.VMEM((POOL_HALO + tm, n_pool), F32)],
        compiler_params=pltpu.CompilerParams(
            dimension_semantics=("arbitrary", "arbitrary"), vmem_limit_bytes=VMEM_LIMIT),
        name="even_out",
    )(attn, u, u, wpool, pscale, wout, x)


def _odd_out_body(o_ref, w_ref, x_ref, y_ref):
    y_ref[...] = x_ref[...] + _dot(o_ref[...], w_ref[...])


def _odd_out(o2, w, x2):
    m, d = x2.shape
    tm = TOKEN_TILE
    return pl.pallas_call(
        _odd_out_body,
        grid=(m // tm,),
        in_specs=[pl.BlockSpec((tm, o2.shape[1]), lambda i: (i, 0)), _resident(w.shape),
                  pl.BlockSpec((tm, d), lambda i: (i, 0))],
        out_specs=pl.BlockSpec((tm, d), lambda i: (i, 0)),
        out_shape=jax.ShapeDtypeStruct((m, d), F32),
        compiler_params=pltpu.CompilerParams(
            dimension_semantics=("arbitrary",), vmem_limit_bytes=VMEM_LIMIT),
        name="odd_out",
    )(o2, w, x2)


def kernel(x, norm_ffn1, ffn1_gate, ffn1_up, ffn1_down, norm_mix, norm_ffn2, ffn2_gate, ffn2_up,
           ffn2_down, even_w_in, even_w_pool, even_pool_scale, even_w_out, odd_w_qkv, odd_w_o,
           norm_final):
    b, s, d = x.shape
    depth = norm_ffn1.shape[0]
    m = b * s
    sb_width = even_w_in.shape[2] - even_w_pool.shape[1] * even_w_pool.shape[2]
    sb_heads = sb_width // (3 * HEAD_DIM)
    moba_heads = odd_w_qkv.shape[2] // (3 * HEAD_DIM)
    gf = norm_final.reshape(1, d)

    x2 = x.reshape(m, d)
    for layer in range(depth):
        i = layer // 2
        x2 = _ffn(x2, norm_ffn1[layer].reshape(1, d), ffn1_gate[layer].astype(BF16),
                  ffn1_up[layer].astype(BF16), ffn1_down[layer].astype(BF16), gf, False)
        gm = norm_mix[layer].reshape(1, d)
        if layer % 2 == 0:
            qkv, u = _proj(x2, gm, even_w_in[i].astype(BF16), sb_width)
            attn = _sb_attention(qkv.reshape(b, s, sb_width), sb_heads)
            x2 = _even_out(attn, u.reshape(b, s, -1), even_w_pool[i].astype(BF16),
                           even_pool_scale[i].reshape(1, -1), even_w_out[i].astype(BF16),
                           x2.reshape(b, s, d)).reshape(m, d)
        else:
            (qkv,) = _proj(x2, gm, odd_w_qkv[i].astype(BF16), odd_w_qkv.shape[2])
            o = _moba_attention(qkv.reshape(b, s, -1), moba_heads)
            x2 = _odd_out(o.reshape(m, -1), odd_w_o[i].astype(BF16), x2)
        x2 = _ffn(x2, norm_ffn2[layer].reshape(1, d), ffn2_gate[layer].astype(BF16),
                  ffn2_up[layer].astype(BF16), ffn2_down[layer].astype(BF16), gf,
                  layer == depth - 1)
    return x2.reshape(b, s, d)
```

```python
import functools

import jax
import jax.numpy as jnp
import numpy as np
from jax import lax
from jax.experimental import pallas as pl
from jax.experimental.pallas import tpu as pltpu

F32 = jnp.float32
BF16 = jnp.bfloat16

LANES = 128
HEAD_DIM = 64
HEADS_PER_BLOCK = LANES // HEAD_DIM
ATT_TILE = 256
MOBA_TOPK = 3
POOL_WINDOWS = (2, 4, 8, 16)
POOL_HALO = 16
RMS_EPS = 1e-6
MASKED = -1e30
TOKEN_TILE = 512
FF_CHUNK = 1024
VMEM_LIMIT = 56 * 1024 * 1024


def _rmsnorm(x, g):
    return x * lax.rsqrt(jnp.mean(x * x, axis=-1, keepdims=True) + RMS_EPS) * g


def _resident(shape):
    nd = len(shape)
    return pl.BlockSpec(shape, lambda *_: (0,) * nd, pipeline_mode=pl.Buffered(1))


def _dot(a, b):
    return jnp.dot(a, b, preferred_element_type=F32)


def _dot_nt(a, b):
    return lax.dot_general(a, b, (((1,), (1,)), ((), ())), preferred_element_type=F32)


def _split_bf16(x):
    hi = x.astype(BF16)
    lo = (x - hi.astype(F32)).astype(BF16)
    return hi, lo


def _ffn_body(x_ref, g_ref, wg_ref, wu_ref, wd_ref, gf_ref, o_ref, *, chunks, final):
    x = x_ref[...]
    h = _rmsnorm(x, g_ref[...]).astype(BF16)
    acc = None
    for c0, c1 in chunks:
        gate = _dot(h, wg_ref[:, c0:c1])
        up = _dot(h, wu_ref[:, c0:c1])
        act = (gate / (1.0 + jnp.exp(-gate)) * up).astype(BF16)
        down = _dot(act, wd_ref[c0:c1, :])
        acc = down if acc is None else acc + down
    y = x + 0.5 * acc
    if final:
        y = _rmsnorm(y, gf_ref[...])
    o_ref[...] = y


def _ffn(x2, g, wg, wu, wd, gf, final):
    m, d = x2.shape
    ff = wg.shape[1]
    chunks = tuple((c, min(c + FF_CHUNK, ff)) for c in range(0, ff, FF_CHUNK))
    tm = TOKEN_TILE
    return pl.pallas_call(
        functools.partial(_ffn_body, chunks=chunks, final=final),
        grid=(m // tm,),
        in_specs=[
            pl.BlockSpec((tm, d), lambda i: (i, 0)),
            _resident((1, d)),
            _resident((d, ff)),
            _resident((d, ff)),
            _resident((ff, d)),
            _resident((1, d)),
        ],
        out_specs=pl.BlockSpec((tm, d), lambda i: (i, 0)),
        out_shape=jax.ShapeDtypeStruct((m, d), F32),
        compiler_params=pltpu.CompilerParams(
            dimension_semantics=("arbitrary",), vmem_limit_bytes=VMEM_LIMIT),
        name="ffn",
    )(x2, g, wg, wu, wd, gf)


def _proj_body(x_ref, g_ref, w_ref, *o_refs, n_bf16):
    h = _rmsnorm(x_ref[...], g_ref[...]).astype(BF16)
    r = _dot(h, w_ref[...])
    o_refs[0][...] = r[:, :n_bf16].astype(BF16)
    if len(o_refs) > 1:
        o_refs[1][...] = r[:, n_bf16:]


def _proj(x2, g, w, n_bf16):
    m, d = x2.shape
    n = w.shape[1]
    tm = TOKEN_TILE
    out_specs = [pl.BlockSpec((tm, n_bf16), lambda i: (i, 0))]
    out_shape = [jax.ShapeDtypeStruct((m, n_bf16), BF16)]
    if n > n_bf16:
        out_specs.append(pl.BlockSpec((tm, n - n_bf16), lambda i: (i, 0)))
        out_shape.append(jax.ShapeDtypeStruct((m, n - n_bf16), F32))
    return pl.pallas_call(
        functools.partial(_proj_body, n_bf16=n_bf16),
        grid=(m // tm,),
        in_specs=[pl.BlockSpec((tm, d), lambda i: (i, 0)), _resident((1, d)), _resident((d, n))],
        out_specs=out_specs,
        out_shape=out_shape,
        compiler_params=pltpu.CompilerParams(
            dimension_semantics=("arbitrary",), vmem_limit_bytes=VMEM_LIMIT),
        name="proj",
    )(x2, g, w)


def _head_mask(hh):
    lane = lax.broadcasted_iota(jnp.int32, (1, LANES), 1)
    return (lane // HEAD_DIM) == hh


def _load_q(q_ref, r0, hmask):
    q = q_ref[0, pl.ds(r0, ATT_TILE), :]
    return jnp.where(hmask, q, jnp.zeros_like(q)) * jnp.asarray(HEAD_DIM ** -0.5, BF16)


def _stage_v_transposed(v_ref, vt_ref, nb):
    for j in range(nb):
        cols = slice(j * ATT_TILE, (j + 1) * ATT_TILE)
        vt_ref[:, cols] = v_ref[0, cols, :].astype(F32).T.astype(BF16)


def _emit_output(ot_ref, o_ref, nb):
    for j in range(nb):
        cols = slice(j * ATT_TILE, (j + 1) * ATT_TILE)
        o_ref[0, cols, :] = ot_ref[:, cols].T.astype(o_ref.dtype)


def _software_pipeline(items, stages):
    state = [None] * len(items)
    for step in range(len(items) + len(stages) - 1):
        for s, stage in enumerate(stages):
            x = step - s
            if 0 <= x < len(items):
                state[x] = stage(items[x], state[x])


def _attn_call(body, qkv, n_heads, extra_inputs, extra_specs, name):
    b, s, _ = qkv.shape
    nblk = n_heads // HEADS_PER_BLOCK
    nb = s // ATT_TILE
    blk = (1, s, LANES)
    return pl.pallas_call(
        functools.partial(body, nb=nb),
        grid=(b, nblk),
        in_specs=extra_specs + [
            pl.BlockSpec(blk, lambda b_, p: (b_, 0, p)),
            pl.BlockSpec(blk, lambda b_, p: (b_, 0, nblk + p)),
            pl.BlockSpec(blk, lambda b_, p: (b_, 0, 2 * nblk + p)),
        ],
        out_specs=pl.BlockSpec(blk, lambda b_, p: (b_, 0, p)),
        out_shape=jax.ShapeDtypeStruct((b, s, nblk * LANES), BF16),
        scratch_shapes=[pltpu.VMEM((LANES, s), BF16), pltpu.VMEM((LANES, s), F32)],
        compiler_params=pltpu.CompilerParams(
            dimension_semantics=("arbitrary", "arbitrary"), vmem_limit_bytes=VMEM_LIMIT),
        name=name,
    )(*extra_inputs, qkv, qkv, qkv)


def _sb_body(q_ref, k_ref, v_ref, o_ref, vt_ref, ot_ref, *, nb):
    t = ATT_TILE
    _stage_v_transposed(v_ref, vt_ref, nb)
    s_r = lax.broadcasted_iota(jnp.int32, (t, t), 0)
    t_r = lax.broadcasted_iota(jnp.int32, (t, t), 1)
    past = s_r < t_r
    tri = (t_r > s_r).astype(BF16)

    def head(hh, carry):
        hmask = _head_mask(hh)
        rows = pl.ds(pl.multiple_of(hh * HEAD_DIM, HEAD_DIM), HEAD_DIM)
        qhs = [_load_q(q_ref, qb * t, hmask) for qb in range(nb)]
        parts = {}

        def scores(tile, _):
            qb, j = tile
            return _dot_nt(k_ref[0, j * t:(j + 1) * t, :], qhs[qb])

        def log_terms(tile, z):
            qb, j = tile
            lg = -(jnp.maximum(z, 0.0) + jnp.log(1.0 + jnp.exp(-jnp.abs(z))))
            if j == qb:
                lg = jnp.where(past, lg, 0.0)
            hi, lo = _split_bf16(lg)
            return z + lg, lg[0:1, :], jnp.concatenate([hi, lo], axis=1)

        def later_sum(tile, s):
            zl, lg0, hilo = s
            return zl, lg0, _dot(tri, hilo)

        def weights(tile, s):
            qb, j = tile
            zl, lg0, c2 = s
            later = c2[:, :t] + c2[:, t:]
            a = jnp.exp(zl + later)
            if j == qb:
                a = jnp.where(past, a, 0.0)
            return later[0:1, :] + lg0, a.astype(BF16)

        def values(tile, s):
            qb, j = tile
            total, a = s
            parts.setdefault(qb, []).append((total, _dot(vt_ref[rows, j * t:(j + 1) * t], a)))
            if j == qb:
                run, acc = parts[qb][-1]
                for total, pv in reversed(parts[qb][:-1]):
                    acc = acc + jnp.exp(run) * pv
                    run = run + total
                ot_ref[rows, qb * t:(qb + 1) * t] = acc

        tiles = [(qb, j) for qb in range(nb) for j in range(qb + 1)]
        _software_pipeline(tiles, [scores, log_terms, later_sum, weights, values])
        return carry

    lax.fori_loop(0, HEADS_PER_BLOCK, head, 0)
    _emit_output(ot_ref, o_ref, nb)


def _sb_attention(qkv, n_heads):
    return _attn_call(_sb_body, qkv, n_heads, [], [], "sb_attn")


def _moba_body(slopes_ref, q_ref, k_ref, v_ref, o_ref, vt_ref, ot_ref, *, nb):
    t = ATT_TILE
    pair = pl.program_id(1)
    _stage_v_transposed(v_ref, vt_ref, nb)
    s_r = lax.broadcasted_iota(jnp.int32, (t, t), 0)
    t_r = lax.broadcasted_iota(jnp.int32, (t, t), 1)
    dist = (t_r - s_r).astype(F32)
    causal = s_r <= t_r
    blk = lax.broadcasted_iota(jnp.int32, (nb, t), 0)

    kmean = jnp.concatenate(
        [jnp.sum(k_ref[0, j * t:(j + 1) * t, :].astype(F32), axis=0, keepdims=True)
         for j in range(nb)], axis=0) * (1.0 / t)

    def head(hh, carry):
        hmask = _head_mask(hh)
        rows = pl.ds(pl.multiple_of(hh * HEAD_DIM, HEAD_DIM), HEAD_DIM)
        slope = slopes_ref[pair * HEADS_PER_BLOCK + hh]
        bias = dist * (-slope)
        bias_diag = jnp.where(causal, bias, MASKED)
        km_hi, km_lo = _split_bf16(jnp.where(hmask, kmean, 0.0))
        qhs = [_load_q(q_ref, qb * t, hmask) for qb in range(nb)]

        parts, unused = {}, {}

        def scores(tile, _):
            qb, j = tile
            st = _dot_nt(k_ref[0, j * t:(j + 1) * t, :], qhs[qb])
            return st + (bias_diag if j == qb else bias)

        def softmax(tile, st):
            mx = jnp.max(st, axis=0, keepdims=True)
            p = jnp.exp(st - mx)
            return mx, jnp.sum(p, axis=0, keepdims=True), p.astype(BF16)

        def values(tile, s):
            qb, j = tile
            mx, ls, p = s
            pv = _dot(vt_ref[rows, j * t:(j + 1) * t], p)
            ms = mx - slope * float((qb - j) * t)
            if qb > MOBA_TOPK:
                if j == 0:
                    unused[qb] = unused_blocks(qb)
                if j < qb:
                    ms = ms + unused[qb][j:j + 1, :]
            parts.setdefault(qb, []).append((ms, ls, pv))
            if j == qb:
                merge(qb, parts[qb])

        def unused_blocks(qb):
            gate = _dot_nt(km_hi, qhs[qb]) + _dot_nt(km_lo, qhs[qb])
            rank = jnp.zeros((nb, t), jnp.int32)
            for n in range(qb):
                other = gate[n:n + 1, :]
                beats = (other > gate) | ((other == gate) & (n < blk))
                rank = rank + jnp.where(beats, 1, 0)
            return jnp.where(rank < MOBA_TOPK, 0.0, MASKED)

        def merge(qb, parts):
            m = parts[0][0]
            for ms, _, _ in parts[1:]:
                m = jnp.maximum(m, ms)
            l, acc = None, None
            for ms, ls, pv in parts:
                w = jnp.exp(ms - m)
                l = w * ls if l is None else l + w * ls
                acc = w * pv if acc is None else acc + w * pv
            ot_ref[rows, qb * t:(qb + 1) * t] = acc / l

        tiles = [(qb, j) for qb in range(nb) for j in range(qb + 1)]
        _software_pipeline(tiles, [scores, softmax, values])
        return carry

    lax.fori_loop(0, HEADS_PER_BLOCK, head, 0)
    _emit_output(ot_ref, o_ref, nb)


def _moba_attention(qkv, n_heads):
    slopes = jnp.asarray(2.0 ** (-8.0 * np.arange(1, n_heads + 1) / n_heads), dtype=F32)
    return _attn_call(_moba_body, qkv, n_heads, [slopes],
                      [pl.BlockSpec(memory_space=pltpu.SMEM)], "moba_attn")


def _even_out_body(a_ref, u_ref, uprev_ref, wpool_ref, pscale_ref, wout_ref, x_ref, o_ref,
                   ucat_ref, *, tm, n_attn):
    ti = pl.program_id(1)
    halo = uprev_ref[0]
    ucat_ref[0:POOL_HALO, :] = jnp.where(ti > 0, halo, jnp.zeros_like(halo))
    ucat_ref[POOL_HALO:, :] = u_ref[0]
    pos = ti * tm + lax.broadcasted_iota(jnp.int32, (tm, 1), 0)
    mixed = []
    for g, w in enumerate(POOL_WINDOWS):
        lanes = slice(g * LANES, (g + 1) * LANES)
        win = ucat_ref[POOL_HALO:POOL_HALO + tm, lanes]
        for i in range(1, w):
            win = win + ucat_ref[POOL_HALO - i:POOL_HALO - i + tm, lanes]
        count = jnp.minimum(pos + 1, w).astype(F32)
        pooled = win / count - ucat_ref[POOL_HALO:POOL_HALO + tm, lanes]
        mixed.append(_dot(pooled.astype(BF16), wpool_ref[g]) * pscale_ref[:, lanes])
    p = jnp.concatenate(mixed, axis=1).astype(BF16)
    y = _dot(a_ref[0], wout_ref[:n_attn, :]) + _dot(p, wout_ref[n_attn:, :])
    o_ref[0] = x_ref[0] + y


def _even_out(attn, u, wpool, pscale, wout, x):
    b, s, d = x.shape
    n_attn = attn.shape[2]
    n_pool = u.shape[2]
    tm = TOKEN_TILE
    halo_blocks = tm // POOL_HALO
    return pl.pallas_call(
        functools.partial(_even_out_body, tm=tm, n_attn=n_attn),
        grid=(b, s // tm),
        in_specs=[
            pl.BlockSpec((1, tm, n_attn), lambda b_, i: (b_, i, 0)),
            pl.BlockSpec((1, tm, n_pool), lambda b_, i: (b_, i, 0)),
            pl.BlockSpec((1, POOL_HALO, n_pool),
                         lambda b_, i: (b_, jnp.maximum(i * halo_blocks - 1, 0), 0)),
            _resident(wpool.shape),
            _resident(pscale.shape),
            _resident(wout.shape),
            pl.BlockSpec((1, tm, d), lambda b_, i: (b_, i, 0)),
        ],
        out_specs=pl.BlockSpec((1, tm, d), lambda b_, i: (b_, i, 0)),
        out_shape=jax.ShapeDtypeStruct((b, s, d), F32),
        scratch_shapes=[pltpu.VMEM((POOL_HALO + tm, n_pool), F32)],
        compiler_params=pltpu.CompilerParams(
            dimension_semantics=("arbitrary", "arbitrary"), vmem_limit_bytes=VMEM_LIMIT),
        name="even_out",
    )(attn, u, u, wpool, pscale, wout, x)


def _odd_out_body(o_ref, w_ref, x_ref, y_ref):
    y_ref[...] = x_ref[...] + _dot(o_ref[...], w_ref[...])


def _odd_out(o2, w, x2):
    m, d = x2.shape
    tm = TOKEN_TILE
    return pl.pallas_call(
        _odd_out_body,
        grid=(m // tm,),
        in_specs=[pl.BlockSpec((tm, o2.shape[1]), lambda i: (i, 0)), _resident(w.shape),
                  pl.BlockSpec((tm, d), lambda i: (i, 0))],
        out_specs=pl.BlockSpec((tm, d), lambda i: (i, 0)),
        out_shape=jax.ShapeDtypeStruct((m, d), F32),
        compiler_params=pltpu.CompilerParams(
            dimension_semantics=("arbitrary",), vmem_limit_bytes=VMEM_LIMIT),
        name="odd_out",
    )(o2, w, x2)


def kernel(x, norm_ffn1, ffn1_gate, ffn1_up, ffn1_down, norm_mix, norm_ffn2, ffn2_gate, ffn2_up,
           ffn2_down, even_w_in, even_w_pool, even_pool_scale, even_w_out, odd_w_qkv, odd_w_o,
           norm_final):
    b, s, d = x.shape
    depth = norm_ffn1.shape[0]
    m = b * s
    sb_width = even_w_in.shape[2] - even_w_pool.shape[1] * even_w_pool.shape[2]
    sb_heads = sb_width // (3 * HEAD_DIM)
    moba_heads = odd_w_qkv.shape[2] // (3 * HEAD_DIM)
    gf = norm_final.reshape(1, d)

    x2 = x.reshape(m, d)
    for layer in range(depth):
        i = layer // 2
        x2 = _ffn(x2, norm_ffn1[layer].reshape(1, d), ffn1_gate[layer].astype(BF16),
                  ffn1_up[layer].astype(BF16), ffn1_down[layer].astype(BF16), gf, False)
        gm = norm_mix[layer].reshape(1, d)
        if layer % 2 == 0:
            qkv, u = _proj(x2, gm, even_w_in[i].astype(BF16), sb_width)
            attn = _sb_attention(qkv.reshape(b, s, sb_width), sb_heads)
            x2 = _even_out(attn, u.reshape(b, s, -1), even_w_pool[i].astype(BF16),
                           even_pool_scale[i].reshape(1, -1), even_w_out[i].astype(BF16),
                           x2.reshape(b, s, d)).reshape(m, d)
        else:
            (qkv,) = _proj(x2, gm, odd_w_qkv[i].astype(BF16), odd_w_qkv.shape[2])
            o = _moba_attention(qkv.reshape(b, s, -1), moba_heads)
            x2 = _odd_out(o.reshape(m, -1), odd_w_o[i].astype(BF16), x2)
        x2 = _ffn(x2, norm_ffn2[layer].reshape(1, d), ffn2_gate[layer].astype(BF16),
                  ffn2_up[layer].astype(BF16), ffn2_down[layer].astype(BF16), gf,
                  layer == depth - 1)
    return x2.reshape(b, s, d)
```

```python
import functools

import jax
import jax.numpy as jnp
import numpy as np
from jax import lax
from jax.experimental import pallas as pl
from jax.experimental.pallas import tpu as pltpu

F32 = jnp.float32
BF16 = jnp.bfloat16

LANES = 128
HEAD_DIM = 64
HEADS_PER_BLOCK = LANES // HEAD_DIM
ATT_TILE = 256
MOBA_TOPK = 3
POOL_WINDOWS = (2, 4, 8, 16)
POOL_HALO = 16
RMS_EPS = 1e-6
MASKED = -1e30
LOG2E = 1.4426950408889634
Q_SCALE = LOG2E * HEAD_DIM ** -0.5
ONES_ROWS = 16
TOKEN_TILE = 512
FF_CHUNK = 1024
VMEM_LIMIT = 56 * 1024 * 1024


def _rmsnorm(x, g):
    return x * lax.rsqrt(jnp.mean(x * x, axis=-1, keepdims=True) + RMS_EPS) * g


def _resident(shape):
    nd = len(shape)
    return pl.BlockSpec(shape, lambda *_: (0,) * nd, pipeline_mode=pl.Buffered(1))


def _layer(stack, layer):
    _, r, c = stack.shape
    return pl.BlockSpec((None, r, c), lambda *_: (layer, 0, 0), pipeline_mode=pl.Buffered(1))


def _dot(a, b):
    return jnp.dot(a, b, preferred_element_type=F32)


def _dot_nt(a, b):
    return lax.dot_general(a, b, (((1,), (1,)), ((), ())), preferred_element_type=F32)


def _split_bf16(x):
    hi = x.astype(BF16)
    lo = (x - hi.astype(F32)).astype(BF16)
    return hi, lo


def _ffn_body(x_ref, g_ref, wg_ref, wu_ref, wd_ref, gf_ref, o_ref, *, chunks, final):
    x = x_ref[...]
    h = _rmsnorm(x, g_ref[...]).astype(BF16)
    acc = None
    for c0, c1 in chunks:
        gate = _dot(h, wg_ref[:, c0:c1])
        up = _dot(h, wu_ref[:, c0:c1])
        act = (gate / (1.0 + jnp.exp(-gate)) * up).astype(BF16)
        down = _dot(act, wd_ref[c0:c1, :])
        acc = down if acc is None else acc + down
    y = x + 0.5 * acc
    if final:
        y = _rmsnorm(y, gf_ref[...])
    o_ref[...] = y


def _ffn(x2, g, wg, wu, wd, gf, layer, final):
    m, d = x2.shape
    ff = wg.shape[2]
    chunks = tuple((c, min(c + FF_CHUNK, ff)) for c in range(0, ff, FF_CHUNK))
    tm = TOKEN_TILE
    return pl.pallas_call(
        functools.partial(_ffn_body, chunks=chunks, final=final),
        grid=(m // tm,),
        in_specs=[
            pl.BlockSpec((tm, d), lambda t: (t, 0)),
            _layer(g, layer),
            _layer(wg, layer),
            _layer(wu, layer),
            _layer(wd, layer),
            _resident((1, d)),
        ],
        out_specs=pl.BlockSpec((tm, d), lambda t: (t, 0)),
        out_shape=jax.ShapeDtypeStruct((m, d), F32),
        compiler_params=pltpu.CompilerParams(
            dimension_semantics=("arbitrary",), vmem_limit_bytes=VMEM_LIMIT),
        name="ffn",
    )(x2, g, wg, wu, wd, gf)


def _proj_body(x_ref, g_ref, w_ref, *o_refs, n_q, n_bf16):
    h = _rmsnorm(x_ref[...], g_ref[...]).astype(BF16)
    r = _dot(h, w_ref[...])
    o_refs[0][:, :n_q] = (r[:, :n_q] * Q_SCALE).astype(BF16)
    o_refs[0][:, n_q:] = r[:, n_q:n_bf16].astype(BF16)
    if len(o_refs) > 1:
        o_refs[1][...] = r[:, n_bf16:]


def _proj(x2, g, w, layer, i, n_q, n_bf16):
    m, d = x2.shape
    n = w.shape[2]
    tm = TOKEN_TILE
    out_specs = [pl.BlockSpec((tm, n_bf16), lambda t: (t, 0))]
    out_shape = [jax.ShapeDtypeStruct((m, n_bf16), BF16)]
    if n > n_bf16:
        out_specs.append(pl.BlockSpec((tm, n - n_bf16), lambda t: (t, 0)))
        out_shape.append(jax.ShapeDtypeStruct((m, n - n_bf16), F32))
    return pl.pallas_call(
        functools.partial(_proj_body, n_q=n_q, n_bf16=n_bf16),
        grid=(m // tm,),
        in_specs=[pl.BlockSpec((tm, d), lambda t: (t, 0)), _layer(g, layer), _layer(w, i)],
        out_specs=out_specs,
        out_shape=out_shape,
        compiler_params=pltpu.CompilerParams(
            dimension_semantics=("arbitrary",), vmem_limit_bytes=VMEM_LIMIT),
        name="proj",
    )(x2, g, w)


def _head_mask(hh):
    lane = lax.broadcasted_iota(jnp.int32, (1, LANES), 1)
    return (lane // HEAD_DIM) == hh


def _load_q(q_ref, r0, hmask, other_lanes):
    q = q_ref[0, pl.ds(r0, ATT_TILE), :]
    return jnp.where(hmask, q, other_lanes)


def _stage_v_transposed(v_ref, vt_ref, nb):
    seq = vt_ref.shape[2]
    ones_row = lax.broadcasted_iota(jnp.int32, (ONES_ROWS, seq), 0) == 0
    for h in range(HEADS_PER_BLOCK):
        vt_ref[h, HEAD_DIM:, :] = jnp.where(ones_row, 1.0, 0.0).astype(BF16)
    for j in range(nb):
        cols = slice(j * ATT_TILE, (j + 1) * ATT_TILE)
        vt = v_ref[0, cols, :].astype(F32).T.astype(BF16)
        for h in range(HEADS_PER_BLOCK):
            vt_ref[h, :HEAD_DIM, cols] = vt[h * HEAD_DIM:(h + 1) * HEAD_DIM, :]


def _emit_output(ot_ref, o_ref, nb):
    for j in range(nb):
        cols = slice(j * ATT_TILE, (j + 1) * ATT_TILE)
        o_ref[0, cols, :] = ot_ref[:, cols].T.astype(o_ref.dtype)


def _software_pipeline(items, stages, width):
    groups = [items[i:i + width] for i in range(0, len(items), width)]
    state = {}
    for step in range(len(groups) + len(stages) - 1):
        for s, stage in enumerate(stages):
            x = step - s
            if 0 <= x < len(groups):
                for item in groups[x]:
                    state[item] = stage(item, state.get(item))


def _attn_call(body, qkv, n_heads, extra_inputs, extra_specs, extra_scratch, name):
    b, s, _ = qkv.shape
    nblk = n_heads // HEADS_PER_BLOCK
    nb = s // ATT_TILE
    blk = (1, s, LANES)
    return pl.pallas_call(
        functools.partial(body, nb=nb),
        grid=(b, nblk),
        in_specs=extra_specs + [
            pl.BlockSpec(blk, lambda b_, p: (b_, 0, p)),
            pl.BlockSpec(blk, lambda b_, p: (b_, 0, nblk + p)),
            pl.BlockSpec(blk, lambda b_, p: (b_, 0, 2 * nblk + p)),
        ],
        out_specs=pl.BlockSpec(blk, lambda b_, p: (b_, 0, p)),
        out_shape=jax.ShapeDtypeStruct((b, s, nblk * LANES), BF16),
        scratch_shapes=[pltpu.VMEM((HEADS_PER_BLOCK, HEAD_DIM + ONES_ROWS, s), BF16),
                        pltpu.VMEM((LANES, s), F32)] + extra_scratch,
        compiler_params=pltpu.CompilerParams(
            dimension_semantics=("arbitrary", "arbitrary"), vmem_limit_bytes=VMEM_LIMIT),
        name=name,
    )(*extra_inputs, qkv, qkv, qkv)


def _sb_body(q_ref, k_ref, v_ref, o_ref, vt_ref, ot_ref, *, nb):
    t = ATT_TILE
    _stage_v_transposed(v_ref, vt_ref, nb)
    s_r = lax.broadcasted_iota(jnp.int32, (t, t), 0)
    t_r = lax.broadcasted_iota(jnp.int32, (t, t), 1)
    past = s_r < t_r
    tri = (t_r > s_r).astype(BF16)
    no_lanes = jnp.zeros((1, LANES), BF16)
    qhs = {(hh, qb): _load_q(q_ref, qb * t, _head_mask(hh), no_lanes)
           for hh in range(HEADS_PER_BLOCK) for qb in range(nb)}
    parts = {}

    def scores(tile, _):
        hh, qb, j = tile
        return _dot_nt(k_ref[0, j * t:(j + 1) * t, :], qhs[hh, qb])

    def log_terms(tile, z):
        hh, qb, j = tile
        zl = jnp.minimum(z, 0.0) - jnp.log2(1.0 + jnp.exp2(-jnp.abs(z)))
        lg = zl - z
        if j == qb:
            lg = jnp.where(past, lg, 0.0)
        return zl, lg[0:1, :], lg.astype(BF16)

    def later_sum(tile, s):
        zl, lg0, lgb = s
        return zl, lg0, _dot(tri, lgb)

    def weights(tile, s):
        hh, qb, j = tile
        zl, lg0, later = s
        a = jnp.exp2(zl + later)
        if j == qb:
            a = jnp.where(past, a, 0.0)
        return later[0:1, :] + lg0, a.astype(BF16)

    def values(tile, s):
        hh, qb, j = tile
        total, a = s
        pv = _dot(vt_ref[hh, :HEAD_DIM, j * t:(j + 1) * t], a)
        parts.setdefault((hh, qb), []).append((total, pv))
        if j == qb:
            run, acc = parts[hh, qb][-1]
            for total, pv in reversed(parts[hh, qb][:-1]):
                acc = acc + jnp.exp2(run) * pv
                run = run + total
            ot_ref[hh * HEAD_DIM:(hh + 1) * HEAD_DIM, qb * t:(qb + 1) * t] = acc

    tiles = [(hh, qb, j) for qb in range(nb) for j in range(qb + 1)
             for hh in range(HEADS_PER_BLOCK)]
    _software_pipeline(
        tiles,
        [lambda tile, s: log_terms(tile, scores(tile, s)),
         lambda tile, s: weights(tile, later_sum(tile, s)),
         values],
        HEADS_PER_BLOCK)
    _emit_output(ot_ref, o_ref, nb)


def _sb_attention(qkv, n_heads):
    return _attn_call(_sb_body, qkv, n_heads, [], [], [], "sb_attn")


def _moba_body(slopes_ref, q_ref, k_ref, v_ref, o_ref, vt_ref, ot_ref, ka_ref, *, nb):
    t = ATT_TILE
    pair = pl.program_id(1)
    _stage_v_transposed(v_ref, vt_ref, nb)
    causal = (lax.broadcasted_iota(jnp.int32, (t, t), 0)
              <= lax.broadcasted_iota(jnp.int32, (t, t), 1))
    blk = lax.broadcasted_iota(jnp.int32, (nb, t), 0)
    lane = lax.broadcasted_iota(jnp.int32, (1, LANES), 1)
    key_offset = lax.broadcasted_iota(jnp.int32, (t, LANES), 0).astype(F32).astype(BF16)

    kmean = jnp.concatenate(
        [jnp.sum(k_ref[0, j * t:(j + 1) * t, :].astype(F32), axis=0, keepdims=True)
         for j in range(nb)], axis=0) * (1.0 / t)

    slopes, kms, qhs = {}, {}, {}
    for hh in range(HEADS_PER_BLOCK):
        hmask = _head_mask(hh)
        slope = slopes_ref[pair * HEADS_PER_BLOCK + hh] * LOG2E
        spare = (1 - hh) * HEAD_DIM
        s0 = jnp.full((1, LANES), slope, F32)
        s1 = s0.astype(BF16).astype(F32)
        s2 = (s0 - s1).astype(BF16).astype(F32)
        s3 = s0 - s1 - s2
        slope_lanes = jnp.where(lane == spare, s1, jnp.where(
            lane == spare + 1, s2, jnp.where(lane == spare + 2, s3, 0.0))).astype(BF16)
        offset_lanes = (lane >= spare) & (lane < spare + 3)
        for j in range(nb):
            blk_rows = slice(j * t, (j + 1) * t)
            ka_ref[hh, blk_rows, :] = jnp.where(offset_lanes, key_offset, k_ref[0, blk_rows, :])
        slopes[hh] = slope
        kms[hh] = _split_bf16(jnp.where(hmask, kmean, 0.0))
        for qb in range(nb):
            qhs[hh, qb] = _load_q(q_ref, qb * t, hmask, slope_lanes)
    parts, unused = {}, {}

    def scores(tile, _):
        hh, qb, j = tile
        st = _dot_nt(ka_ref[hh, j * t:(j + 1) * t, :], qhs[hh, qb])
        return jnp.where(causal, st, MASKED) if j == qb else st

    def softmax(tile, st):
        mx = jnp.max(st, axis=0, keepdims=True)
        return mx, jnp.exp2(st - mx).astype(BF16)

    def values(tile, s):
        hh, qb, j = tile
        mx, p = s
        pv = _dot(vt_ref[hh, :, j * t:(j + 1) * t], p)
        ms = mx - slopes[hh] * float((qb - j) * t)
        if qb > MOBA_TOPK:
            if j == 0:
                unused[hh, qb] = unused_blocks(hh, qb)
            if j < qb:
                ms = ms + unused[hh, qb][j:j + 1, :]
        parts.setdefault((hh, qb), []).append((ms, pv))
        if j == qb:
            merge(hh, qb, parts[hh, qb])

    def unused_blocks(hh, qb):
        km_hi, km_lo = kms[hh]
        gate = _dot_nt(km_hi, qhs[hh, qb]) + _dot_nt(km_lo, qhs[hh, qb])
        rank = jnp.zeros((nb, t), jnp.int32)
        for n in range(qb):
            other = gate[n:n + 1, :]
            beats = (other > gate) | ((other == gate) & (n < blk))
            rank = rank + jnp.where(beats, 1, 0)
        return jnp.where(rank < MOBA_TOPK, 0.0, MASKED)

    def merge(hh, qb, tile_parts):
        m = tile_parts[0][0]
        for ms, _ in tile_parts[1:]:
            m = jnp.maximum(m, ms)
        acc = None
        for ms, pv in tile_parts:
            w = jnp.exp2(ms - m)
            acc = w * pv if acc is None else acc + w * pv
        ot_ref[hh * HEAD_DIM:(hh + 1) * HEAD_DIM, qb * t:(qb + 1) * t] = (
            acc[:HEAD_DIM, :] / acc[HEAD_DIM:HEAD_DIM + 1, :])

    tiles = [(hh, qb, j) for qb in range(nb) for j in range(qb + 1)
             for hh in range(HEADS_PER_BLOCK)]
    _software_pipeline(tiles, [scores, softmax, values], HEADS_PER_BLOCK)
    _emit_output(ot_ref, o_ref, nb)


def _moba_attention(qkv, n_heads):
    slopes = jnp.asarray(2.0 ** (-8.0 * np.arange(1, n_heads + 1) / n_heads), dtype=F32)
    return _attn_call(_moba_body, qkv, n_heads, [slopes], [pl.BlockSpec(memory_space=pltpu.SMEM)],
                      [pltpu.VMEM((HEADS_PER_BLOCK, qkv.shape[1], LANES), BF16)], "moba_attn")


def _even_out_body(a_ref, u_ref, uprev_ref, wpool_ref, pscale_ref, wout_ref, x_ref, o_ref,
                   ucat_ref, *, tm, n_attn):
    ti = pl.program_id(1)
    halo = uprev_ref[0]
    ucat_ref[0:POOL_HALO, :] = jnp.where(ti > 0, halo, jnp.zeros_like(halo))
    ucat_ref[POOL_HALO:, :] = u_ref[0]
    pos = ti * tm + lax.broadcasted_iota(jnp.int32, (tm, 1), 0)
    mixed = []
    for g, w in enumerate(POOL_WINDOWS):
        lanes = slice(g * LANES, (g + 1) * LANES)
        win = ucat_ref[POOL_HALO:POOL_HALO + tm, lanes]
        for i in range(1, w):
            win = win + ucat_ref[POOL_HALO - i:POOL_HALO - i + tm, lanes]
        count = jnp.minimum(pos + 1, w).astype(F32)
        pooled = win / count - ucat_ref[POOL_HALO:POOL_HALO + tm, lanes]
        mixed.append(_dot(pooled.astype(BF16), wpool_ref[g]) * pscale_ref[:, lanes])
    p = jnp.concatenate(mixed, axis=1).astype(BF16)
    y = _dot(a_ref[0], wout_ref[:n_attn, :]) + _dot(p, wout_ref[n_attn:, :])
    o_ref[0] = x_ref[0] + y


def _even_out(attn, u, wpool, pscale, wout, i, x):
    b, s, d = x.shape
    n_attn = attn.shape[2]
    n_pool = u.shape[2]
    tm = TOKEN_TILE
    halo_blocks = tm // POOL_HALO
    return pl.pallas_call(
        functools.partial(_even_out_body, tm=tm, n_attn=n_attn),
        grid=(b, s // tm),
        in_specs=[
            pl.BlockSpec((1, tm, n_attn), lambda b_, t: (b_, t, 0)),
            pl.BlockSpec((1, tm, n_pool), lambda b_, t: (b_, t, 0)),
            pl.BlockSpec((1, POOL_HALO, n_pool),
                         lambda b_, t: (b_, jnp.maximum(t * halo_blocks - 1, 0), 0)),
            pl.BlockSpec((None,) + wpool.shape[1:], lambda *_: (i, 0, 0, 0),
                         pipeline_mode=pl.Buffered(1)),
            _layer(pscale, i),
            _layer(wout, i),
            pl.BlockSpec((1, tm, d), lambda b_, t: (b_, t, 0)),
        ],
        out_specs=pl.BlockSpec((1, tm, d), lambda b_, t: (b_, t, 0)),
        out_shape=jax.ShapeDtypeStruct((b, s, d), F32),
        scratch_shapes=[pltpu.VMEM((POOL_HALO + tm, n_pool), F32)],
        compiler_params=pltpu.CompilerParams(
            dimension_semantics=("arbitrary", "arbitrary"), vmem_limit_bytes=VMEM_LIMIT),
        name="even_out",
    )(attn, u, u, wpool, pscale, wout, x)


def _odd_out_body(o_ref, w_ref, x_ref, y_ref):
    y_ref[...] = x_ref[...] + _dot(o_ref[...], w_ref[...])


def _odd_out(o2, w, i, x2):
    m, d = x2.shape
    tm = TOKEN_TILE
    return pl.pallas_call(
        _odd_out_body,
        grid=(m // tm,),
        in_specs=[pl.BlockSpec((tm, o2.shape[1]), lambda t: (t, 0)), _layer(w, i),
                  pl.BlockSpec((tm, d), lambda t: (t, 0))],
        out_specs=pl.BlockSpec((tm, d), lambda t: (t, 0)),
        out_shape=jax.ShapeDtypeStruct((m, d), F32),
        compiler_params=pltpu.CompilerParams(
            dimension_semantics=("arbitrary",), vmem_limit_bytes=VMEM_LIMIT),
        name="odd_out",
    )(o2, w, x2)


def kernel(x, norm_ffn1, ffn1_gate, ffn1_up, ffn1_down, norm_mix, norm_ffn2, ffn2_gate, ffn2_up,
           ffn2_down, even_w_in, even_w_pool, even_pool_scale, even_w_out, odd_w_qkv, odd_w_o,
           norm_final):
    b, s, d = x.shape
    depth = norm_ffn1.shape[0]
    m = b * s
    sb_width = even_w_in.shape[2] - even_w_pool.shape[1] * even_w_pool.shape[2]
    sb_heads = sb_width // (3 * HEAD_DIM)
    moba_heads = odd_w_qkv.shape[2] // (3 * HEAD_DIM)
    gf = norm_final.reshape(1, d)
    g1, gm, g2 = (g.reshape(depth, 1, d) for g in (norm_ffn1, norm_mix, norm_ffn2))
    w1 = [w.astype(BF16) for w in (ffn1_gate, ffn1_up, ffn1_down)]
    w2 = [w.astype(BF16) for w in (ffn2_gate, ffn2_up, ffn2_down)]
    w_in, w_pool, w_out = (w.astype(BF16) for w in (even_w_in, even_w_pool, even_w_out))
    w_qkv, w_o = odd_w_qkv.astype(BF16), odd_w_o.astype(BF16)
    pool_scale = even_pool_scale.reshape(even_pool_scale.shape[0], 1, -1)

    x2 = x.reshape(m, d)
    for layer in range(depth):
        i = layer // 2
        x2 = _ffn(x2, g1, *w1, gf, layer, False)
        if layer % 2 == 0:
            qkv, u = _proj(x2, gm, w_in, layer, i, sb_width // 3, sb_width)
            attn = _sb_attention(qkv.reshape(b, s, sb_width), sb_heads)
            x2 = _even_out(attn, u.reshape(b, s, -1), w_pool, pool_scale, w_out, i,
                           x2.reshape(b, s, d)).reshape(m, d)
        else:
            n_qkv = w_qkv.shape[2]
            (qkv,) = _proj(x2, gm, w_qkv, layer, i, n_qkv // 3, n_qkv)
            o = _moba_attention(qkv.reshape(b, s, n_qkv), moba_heads)
            x2 = _odd_out(o.reshape(m, -1), w_o, i, x2)
        x2 = _ffn(x2, g2, *w2, gf, layer, layer == depth - 1)
    return x2.reshape(b, s, d)
```

```python
import functools

import jax
import jax.numpy as jnp
import numpy as np
from jax import lax
from jax.experimental import pallas as pl
from jax.experimental.pallas import tpu as pltpu

F32 = jnp.float32
BF16 = jnp.bfloat16

LANES = 128
HEAD_DIM = 64
HEADS_PER_BLOCK = LANES // HEAD_DIM
ATT_TILE = 256
MOBA_TOPK = 3
POOL_WINDOWS = (2, 4, 8, 16)
POOL_HALO = 16
RMS_EPS = 1e-6
MASKED = -1e30
LOG2E = 1.4426950408889634
Q_SCALE = LOG2E * HEAD_DIM ** -0.5
ONES_ROWS = 16
TOKEN_TILE = 512
FF_CHUNK = 1024
PROJ_CHUNK = 1024
VMEM_LIMIT = 56 * 1024 * 1024


def _rmsnorm(x, g):
    return x * lax.rsqrt(jnp.mean(x * x, axis=-1, keepdims=True) + RMS_EPS) * g


def _resident(shape):
    nd = len(shape)
    return pl.BlockSpec(shape, lambda *_: (0,) * nd, pipeline_mode=pl.Buffered(1))


def _layer(stack, layer):
    _, r, c = stack.shape
    return pl.BlockSpec((None, r, c), lambda *_: (layer, 0, 0), pipeline_mode=pl.Buffered(1))


def _dot(a, b):
    return jnp.dot(a, b, preferred_element_type=F32)


def _split_bf16(x):
    hi = x.astype(BF16)
    lo = (x - hi.astype(F32)).astype(BF16)
    return hi, lo


def _swiglu_residual(x, g_ref, wg_ref, wu_ref, wd_ref):
    ff = wg_ref.shape[1]
    h = _rmsnorm(x, g_ref[...]).astype(BF16)
    acc = None
    for c0 in range(0, ff, FF_CHUNK):
        c1 = min(c0 + FF_CHUNK, ff)
        gate = _dot(h, wg_ref[:, c0:c1])
        up = _dot(h, wu_ref[:, c0:c1])
        act = (gate / (1.0 + jnp.exp(-gate)) * up).astype(BF16)
        down = _dot(act, wd_ref[c0:c1, :])
        acc = down if acc is None else acc + down
    return x + 0.5 * acc


def _ffn_weight_specs(g, wg, wu, wd, layer):
    return [_layer(g, layer), _layer(wg, layer), _layer(wu, layer), _layer(wd, layer)]


def _ffn_proj_body(x_ref, g_ref, wg_ref, wu_ref, wd_ref, gm_ref, w_ref, y_ref, qkv_ref, *u_refs,
                   n_q):
    y = _swiglu_residual(x_ref[...], g_ref, wg_ref, wu_ref, wd_ref)
    y_ref[...] = y
    h = _rmsnorm(y, gm_ref[...]).astype(BF16)
    n, n_bf16 = w_ref.shape[1], qkv_ref.shape[1]
    for c0 in range(0, n, PROJ_CHUNK):
        c1 = min(c0 + PROJ_CHUNK, n)
        r = _dot(h, w_ref[:, c0:c1])
        lo, hi = c0, min(c1, n_q)
        if lo < hi:
            qkv_ref[:, lo:hi] = (r[:, lo - c0:hi - c0] * Q_SCALE).astype(BF16)
        lo, hi = max(c0, n_q), min(c1, n_bf16)
        if lo < hi:
            qkv_ref[:, lo:hi] = r[:, lo - c0:hi - c0].astype(BF16)
        lo, hi = max(c0, n_bf16), c1
        if lo < hi:
            u_refs[0][:, lo - n_bf16:hi - n_bf16] = r[:, lo - c0:hi - c0]


def _ffn_proj(x2, ffn, gm, w, layer, i, n_q, n_bf16):
    m, d = x2.shape
    n = w.shape[2]
    tm = TOKEN_TILE
    row_block = lambda width: pl.BlockSpec((tm, width), lambda t: (t, 0))
    out_specs = [row_block(d), row_block(n_bf16)]
    out_shape = [jax.ShapeDtypeStruct((m, d), F32), jax.ShapeDtypeStruct((m, n_bf16), BF16)]
    if n > n_bf16:
        out_specs.append(row_block(n - n_bf16))
        out_shape.append(jax.ShapeDtypeStruct((m, n - n_bf16), F32))
    return pl.pallas_call(
        functools.partial(_ffn_proj_body, n_q=n_q),
        grid=(m // tm,),
        in_specs=[row_block(d)] + _ffn_weight_specs(*ffn, layer) + [_layer(gm, layer), _layer(w, i)],
        out_specs=out_specs,
        out_shape=out_shape,
        compiler_params=pltpu.CompilerParams(
            dimension_semantics=("arbitrary",), vmem_limit_bytes=VMEM_LIMIT),
        name="ffn_proj",
    )(x2, *ffn, gm, w)


def _head_mask(hh):
    lane = lax.broadcasted_iota(jnp.int32, (1, LANES), 1)
    return (lane // HEAD_DIM) == hh


def _load_q(q_ref, r0, hmask, other_lanes):
    q = q_ref[0, pl.ds(r0, ATT_TILE), :]
    return jnp.where(hmask, q, other_lanes).astype(F32).T.astype(BF16)


def _stage_v_transposed(v_ref, vt_ref, nb):
    seq = vt_ref.shape[2]
    ones_row = lax.broadcasted_iota(jnp.int32, (ONES_ROWS, seq), 0) == 0
    for h in range(HEADS_PER_BLOCK):
        vt_ref[h, HEAD_DIM:, :] = jnp.where(ones_row, 1.0, 0.0).astype(BF16)
    for j in range(nb):
        cols = slice(j * ATT_TILE, (j + 1) * ATT_TILE)
        vt = v_ref[0, cols, :].astype(F32).T.astype(BF16)
        for h in range(HEADS_PER_BLOCK):
            vt_ref[h, :HEAD_DIM, cols] = vt[h * HEAD_DIM:(h + 1) * HEAD_DIM, :]


def _emit_output(ot_ref, o_ref, nb):
    for j in range(nb):
        cols = slice(j * ATT_TILE, (j + 1) * ATT_TILE)
        o_ref[0, cols, :] = ot_ref[:, cols].T.astype(o_ref.dtype)


def _software_pipeline(items, stages, width):
    groups = [items[i:i + width] for i in range(0, len(items), width)]
    state = {}
    for step in range(len(groups) + len(stages) - 1):
        for s, stage in enumerate(stages):
            x = step - s
            if 0 <= x < len(groups):
                for item in groups[x]:
                    state[item] = stage(item, state.get(item))


def _attn_call(body, qkv, n_heads, extra_inputs, extra_specs, extra_scratch, name):
    b, s, _ = qkv.shape
    nblk = n_heads // HEADS_PER_BLOCK
    nb = s // ATT_TILE
    blk = (1, s, LANES)
    return pl.pallas_call(
        functools.partial(body, nb=nb),
        grid=(b, nblk),
        in_specs=extra_specs + [
            pl.BlockSpec(blk, lambda b_, p: (b_, 0, p)),
            pl.BlockSpec(blk, lambda b_, p: (b_, 0, nblk + p)),
            pl.BlockSpec(blk, lambda b_, p: (b_, 0, 2 * nblk + p)),
        ],
        out_specs=pl.BlockSpec(blk, lambda b_, p: (b_, 0, p)),
        out_shape=jax.ShapeDtypeStruct((b, s, nblk * LANES), BF16),
        scratch_shapes=[pltpu.VMEM((HEADS_PER_BLOCK, HEAD_DIM + ONES_ROWS, s), BF16),
                        pltpu.VMEM((LANES, s), F32)] + extra_scratch,
        compiler_params=pltpu.CompilerParams(
            dimension_semantics=("arbitrary", "arbitrary"), vmem_limit_bytes=VMEM_LIMIT),
        name=name,
    )(*extra_inputs, qkv, qkv, qkv)


def _sb_body(q_ref, k_ref, v_ref, o_ref, vt_ref, ot_ref, *, nb):
    t = ATT_TILE
    _stage_v_transposed(v_ref, vt_ref, nb)
    s_r = lax.broadcasted_iota(jnp.int32, (t, t), 0)
    t_r = lax.broadcasted_iota(jnp.int32, (t, t), 1)
    past = s_r < t_r
    tri = (t_r > s_r).astype(BF16)
    no_lanes = jnp.zeros((1, LANES), BF16)
    qhs = {(hh, qb): _load_q(q_ref, qb * t, _head_mask(hh), no_lanes)
           for hh in range(HEADS_PER_BLOCK) for qb in range(nb)}
    parts = {}

    def scores(tile, _):
        hh, qb, j = tile
        return _dot(k_ref[0, j * t:(j + 1) * t, :], qhs[hh, qb])

    def log_terms(tile, z):
        hh, qb, j = tile
        zl = jnp.minimum(z, 0.0) - jnp.log2(1.0 + jnp.exp2(-jnp.abs(z)))
        lg = zl - z
        if j == qb:
            lg = jnp.where(past, lg, 0.0)
        return zl, lg[0:1, :], lg.astype(BF16)

    def later_sum(tile, s):
        zl, lg0, lgb = s
        return zl, lg0, _dot(tri, lgb)

    def weights(tile, s):
        hh, qb, j = tile
        zl, lg0, later = s
        a = jnp.exp2(zl + later)
        if j == qb:
            a = jnp.where(past, a, 0.0)
        return later[0:1, :] + lg0, a.astype(BF16)

    def values(tile, s):
        hh, qb, j = tile
        total, a = s
        pv = _dot(vt_ref[hh, :HEAD_DIM, j * t:(j + 1) * t], a)
        parts.setdefault((hh, qb), []).append((total, pv))
        if j == qb:
            run, acc = parts[hh, qb][-1]
            for total, pv in reversed(parts[hh, qb][:-1]):
                acc = acc + jnp.exp2(run) * pv
                run = run + total
            ot_ref[hh * HEAD_DIM:(hh + 1) * HEAD_DIM, qb * t:(qb + 1) * t] = acc

    tiles = [(hh, qb, j) for qb in range(nb) for j in range(qb + 1)
             for hh in range(HEADS_PER_BLOCK)]
    _software_pipeline(
        tiles,
        [lambda tile, s: log_terms(tile, scores(tile, s)),
         lambda tile, s: weights(tile, later_sum(tile, s)),
         values],
        HEADS_PER_BLOCK)
    _emit_output(ot_ref, o_ref, nb)


def _sb_attention(qkv, n_heads):
    return _attn_call(_sb_body, qkv, n_heads, [], [], [], "sb_attn")


def _moba_body(slopes_ref, q_ref, k_ref, v_ref, o_ref, vt_ref, ot_ref, ka_ref, *, nb):
    t = ATT_TILE
    pair = pl.program_id(1)
    _stage_v_transposed(v_ref, vt_ref, nb)
    causal = (lax.broadcasted_iota(jnp.int32, (t, t), 0)
              <= lax.broadcasted_iota(jnp.int32, (t, t), 1))
    blk = lax.broadcasted_iota(jnp.int32, (nb, t), 0)
    lane = lax.broadcasted_iota(jnp.int32, (1, LANES), 1)
    key_offset = lax.broadcasted_iota(jnp.int32, (t, LANES), 0).astype(F32).astype(BF16)

    kmean = jnp.concatenate(
        [jnp.sum(k_ref[0, j * t:(j + 1) * t, :].astype(F32), axis=0, keepdims=True)
         for j in range(nb)], axis=0) * (1.0 / t)

    slopes, kms, qhs = {}, {}, {}
    for hh in range(HEADS_PER_BLOCK):
        hmask = _head_mask(hh)
        slope = slopes_ref[pair * HEADS_PER_BLOCK + hh] * LOG2E
        spare = (1 - hh) * HEAD_DIM
        s0 = jnp.full((1, LANES), slope, F32)
        s1 = s0.astype(BF16).astype(F32)
        s2 = (s0 - s1).astype(BF16).astype(F32)
        s3 = s0 - s1 - s2
        slope_lanes = jnp.where(lane == spare, s1, jnp.where(
            lane == spare + 1, s2, jnp.where(lane == spare + 2, s3, 0.0))).astype(BF16)
        offset_lanes = (lane >= spare) & (lane < spare + 3)
        for j in range(nb):
            blk_rows = slice(j * t, (j + 1) * t)
            ka_ref[hh, blk_rows, :] = jnp.where(offset_lanes, key_offset, k_ref[0, blk_rows, :])
        slopes[hh] = slope
        kms[hh] = _split_bf16(jnp.where(hmask, kmean, 0.0))
        for qb in range(nb):
            qhs[hh, qb] = _load_q(q_ref, qb * t, hmask, slope_lanes)
    parts, unused = {}, {}

    def scores(tile, _):
        hh, qb, j = tile
        st = _dot(ka_ref[hh, j * t:(j + 1) * t, :], qhs[hh, qb])
        return jnp.where(causal, st, MASKED) if j == qb else st

    def softmax(tile, st):
        mx = jnp.max(st, axis=0, keepdims=True)
        return mx, jnp.exp2(st - mx).astype(BF16)

    def values(tile, s):
        hh, qb, j = tile
        mx, p = s
        pv = _dot(vt_ref[hh, :, j * t:(j + 1) * t], p)
        ms = mx - slopes[hh] * float((qb - j) * t)
        if qb > MOBA_TOPK:
            if j == 0:
                unused[hh, qb] = unused_blocks(hh, qb)
            if j < qb:
                ms = ms + unused[hh, qb][j:j + 1, :]
        parts.setdefault((hh, qb), []).append((ms, pv))
        if j == qb:
            merge(hh, qb, parts[hh, qb])

    def unused_blocks(hh, qb):
        km_hi, km_lo = kms[hh]
        gate = _dot(km_hi, qhs[hh, qb]) + _dot(km_lo, qhs[hh, qb])
        rank = jnp.zeros((nb, t), jnp.int32)
        for n in range(qb):
            other = gate[n:n + 1, :]
            beats = (other > gate) | ((other == gate) & (n < blk))
            rank = rank + jnp.where(beats, 1, 0)
        return jnp.where(rank < MOBA_TOPK, 0.0, MASKED)

    def merge(hh, qb, tile_parts):
        m = tile_parts[0][0]
        for ms, _ in tile_parts[1:]:
            m = jnp.maximum(m, ms)
        acc = None
        for ms, pv in tile_parts:
            w = jnp.exp2(ms - m)
            acc = w * pv if acc is None else acc + w * pv
        ot_ref[hh * HEAD_DIM:(hh + 1) * HEAD_DIM, qb * t:(qb + 1) * t] = (
            acc[:HEAD_DIM, :] / acc[HEAD_DIM:HEAD_DIM + 1, :])

    tiles = [(hh, qb, j) for qb in range(nb) for j in range(qb + 1)
             for hh in range(HEADS_PER_BLOCK)]
    _software_pipeline(tiles, [scores, softmax, values], HEADS_PER_BLOCK)
    _emit_output(ot_ref, o_ref, nb)


def _moba_attention(qkv, n_heads):
    slopes = jnp.asarray(2.0 ** (-8.0 * np.arange(1, n_heads + 1) / n_heads), dtype=F32)
    return _attn_call(_moba_body, qkv, n_heads, [slopes], [pl.BlockSpec(memory_space=pltpu.SMEM)],
                      [pltpu.VMEM((HEADS_PER_BLOCK, qkv.shape[1], LANES), BF16)], "moba_attn")


def _finish(y, ffn_refs, gf_ref, final):
    z = _swiglu_residual(y, *ffn_refs)
    return _rmsnorm(z, gf_ref[...]) if final else z


def _even_out_ffn_body(a_ref, u_ref, uprev_ref, wpool_ref, pscale_ref, wout_ref, x_ref,
                       g_ref, wg_ref, wu_ref, wd_ref, gf_ref, o_ref, ucat_ref, *, tm, final):
    n_attn = a_ref.shape[2]
    ti = pl.program_id(1)
    halo = uprev_ref[0]
    ucat_ref[0:POOL_HALO, :] = jnp.where(ti > 0, halo, jnp.zeros_like(halo))
    ucat_ref[POOL_HALO:, :] = u_ref[0]
    pos = ti * tm + lax.broadcasted_iota(jnp.int32, (tm, 1), 0)
    mixed = []
    for g, w in enumerate(POOL_WINDOWS):
        lanes = slice(g * LANES, (g + 1) * LANES)
        win = ucat_ref[POOL_HALO:POOL_HALO + tm, lanes]
        for i in range(1, w):
            win = win + ucat_ref[POOL_HALO - i:POOL_HALO - i + tm, lanes]
        count = jnp.minimum(pos + 1, w).astype(F32)
        pooled = win / count - ucat_ref[POOL_HALO:POOL_HALO + tm, lanes]
        mixed.append(_dot(pooled.astype(BF16), wpool_ref[g]) * pscale_ref[:, lanes])
    p = jnp.concatenate(mixed, axis=1).astype(BF16)
    y = x_ref[0] + _dot(a_ref[0], wout_ref[:n_attn, :]) + _dot(p, wout_ref[n_attn:, :])
    o_ref[0] = _finish(y, (g_ref, wg_ref, wu_ref, wd_ref), gf_ref, final)


def _even_out_ffn(attn, u, wpool, pscale, wout, i, x, ffn, gf, layer, final):
    b, s, d = x.shape
    n_attn = attn.shape[2]
    n_pool = u.shape[2]
    tm = TOKEN_TILE
    halo_blocks = tm // POOL_HALO
    return pl.pallas_call(
        functools.partial(_even_out_ffn_body, tm=tm, final=final),
        grid=(b, s // tm),
        in_specs=[
            pl.BlockSpec((1, tm, n_attn), lambda b_, t: (b_, t, 0)),
            pl.BlockSpec((1, tm, n_pool), lambda b_, t: (b_, t, 0)),
            pl.BlockSpec((1, POOL_HALO, n_pool),
                         lambda b_, t: (b_, jnp.maximum(t * halo_blocks - 1, 0), 0)),
            pl.BlockSpec((None,) + wpool.shape[1:], lambda *_: (i, 0, 0, 0),
                         pipeline_mode=pl.Buffered(1)),
            _layer(pscale, i),
            _layer(wout, i),
            pl.BlockSpec((1, tm, d), lambda b_, t: (b_, t, 0)),
        ] + _ffn_weight_specs(*ffn, layer) + [_resident((1, d))],
        out_specs=pl.BlockSpec((1, tm, d), lambda b_, t: (b_, t, 0)),
        out_shape=jax.ShapeDtypeStruct((b, s, d), F32),
        scratch_shapes=[pltpu.VMEM((POOL_HALO + tm, n_pool), F32)],
        compiler_params=pltpu.CompilerParams(
            dimension_semantics=("arbitrary", "arbitrary"), vmem_limit_bytes=VMEM_LIMIT),
        name="even_out_ffn",
    )(attn, u, u, wpool, pscale, wout, x, *ffn, gf)


def _odd_out_ffn_body(o_ref, w_ref, x_ref, g_ref, wg_ref, wu_ref, wd_ref, gf_ref, y_ref, *, final):
    y = x_ref[...] + _dot(o_ref[...], w_ref[...])
    y_ref[...] = _finish(y, (g_ref, wg_ref, wu_ref, wd_ref), gf_ref, final)


def _odd_out_ffn(o2, w, i, x2, ffn, gf, layer, final):
    m, d = x2.shape
    tm = TOKEN_TILE
    return pl.pallas_call(
        functools.partial(_odd_out_ffn_body, final=final),
        grid=(m // tm,),
        in_specs=[pl.BlockSpec((tm, o2.shape[1]), lambda t: (t, 0)), _layer(w, i),
                  pl.BlockSpec((tm, d), lambda t: (t, 0))]
        + _ffn_weight_specs(*ffn, layer) + [_resident((1, d))],
        out_specs=pl.BlockSpec((tm, d), lambda t: (t, 0)),
        out_shape=jax.ShapeDtypeStruct((m, d), F32),
        compiler_params=pltpu.CompilerParams(
            dimension_semantics=("arbitrary",), vmem_limit_bytes=VMEM_LIMIT),
        name="odd_out_ffn",
    )(o2, w, x2, *ffn, gf)


def kernel(x, norm_ffn1, ffn1_gate, ffn1_up, ffn1_down, norm_mix, norm_ffn2, ffn2_gate, ffn2_up,
           ffn2_down, even_w_in, even_w_pool, even_pool_scale, even_w_out, odd_w_qkv, odd_w_o,
           norm_final):
    b, s, d = x.shape
    depth = norm_ffn1.shape[0]
    m = b * s
    sb_width = even_w_in.shape[2] - even_w_pool.shape[1] * even_w_pool.shape[2]
    sb_heads = sb_width // (3 * HEAD_DIM)
    moba_heads = odd_w_qkv.shape[2] // (3 * HEAD_DIM)
    gf = norm_final.reshape(1, d)
    g1, gm, g2 = (g.reshape(depth, 1, d) for g in (norm_ffn1, norm_mix, norm_ffn2))
    ffn1 = [g1] + [w.astype(BF16) for w in (ffn1_gate, ffn1_up, ffn1_down)]
    ffn2 = [g2] + [w.astype(BF16) for w in (ffn2_gate, ffn2_up, ffn2_down)]
    w_in, w_pool, w_out = (w.astype(BF16) for w in (even_w_in, even_w_pool, even_w_out))
    w_qkv, w_o = odd_w_qkv.astype(BF16), odd_w_o.astype(BF16)
    pool_scale = even_pool_scale.reshape(even_pool_scale.shape[0], 1, -1)

    x2 = x.reshape(m, d)
    for layer in range(depth):
        i = layer // 2
        final = layer == depth - 1
        if layer % 2 == 0:
            x2, qkv, u = _ffn_proj(x2, ffn1, gm, w_in, layer, i, sb_width // 3, sb_width)
            attn = _sb_attention(qkv.reshape(b, s, sb_width), sb_heads)
            x2 = _even_out_ffn(attn, u.reshape(b, s, -1), w_pool, pool_scale, w_out, i,
                               x2.reshape(b, s, d), ffn2, gf, layer, final).reshape(m, d)
        else:
            n_qkv = w_qkv.shape[2]
            x2, qkv = _ffn_proj(x2, ffn1, gm, w_qkv, layer, i, n_qkv // 3, n_qkv)
            o = _moba_attention(qkv.reshape(b, s, n_qkv), moba_heads)
            x2 = _odd_out_ffn(o.reshape(m, -1), w_o, i, x2, ffn2, gf, layer, final)
    return x2.reshape(b, s, d)
```

```python
import functools

import jax
import jax.numpy as jnp
import numpy as np
from jax import lax
from jax.experimental import pallas as pl
from jax.experimental.pallas import tpu as pltpu

F32 = jnp.float32
BF16 = jnp.bfloat16

LANES = 128
HEAD_DIM = 64
HEADS_PER_BLOCK = LANES // HEAD_DIM
ATT_TILE = 256
MOBA_TOPK = 3
POOL_WINDOWS = (2, 4, 8, 16)
POOL_HALO = 16
RMS_EPS = 1e-6
MASKED = -1e30
LOG2E = 1.4426950408889634
Q_SCALE = LOG2E * HEAD_DIM ** -0.5
ONES_ROWS = 16
TOKEN_TILE = 512
OUT_TOKEN_TILE = 1024
FF_CHUNK = 1024
PROJ_CHUNK = 1024
ROW_BLOCK = 256
VMEM_LIMIT = 56 * 1024 * 1024


def _rmsnorm(x, g):
    return x * lax.rsqrt(jnp.mean(x * x, axis=-1, keepdims=True) + RMS_EPS) * g


def _resident(shape):
    nd = len(shape)
    return pl.BlockSpec(shape, lambda *_: (0,) * nd, pipeline_mode=pl.Buffered(1))


def _layer(stack, layer):
    _, r, c = stack.shape
    return pl.BlockSpec((None, r, c), lambda *_: (layer, 0, 0), pipeline_mode=pl.Buffered(1))


def _dot(a, b):
    return jnp.dot(a, b, preferred_element_type=F32)


def _split_bf16(x):
    hi = x.astype(BF16)
    lo = (x - hi.astype(F32)).astype(BF16)
    return hi, lo


def _ffn_weight_specs(g, wg, wu, wd, layer):
    return [_layer(g, layer), _layer(wg, layer), _layer(wu, layer), _layer(wd, layer)]


def _ffn_stages(g_ref, wg_ref, wu_ref, wd_ref):
    ff = wg_ref.shape[1]

    def norm(r, x):
        return x, _rmsnorm(x, g_ref[...]).astype(BF16), None

    def chunk(c0, c1):
        def stage(r, s):
            x, h, acc = s
            gate = _dot(h, wg_ref[:, c0:c1])
            up = _dot(h, wu_ref[:, c0:c1])
            act = (gate / (1.0 + jnp.exp(-gate)) * up).astype(BF16)
            down = _dot(act, wd_ref[c0:c1, :])
            return x, h, (down if acc is None else acc + down)
        return stage

    def residual(r, s):
        x, _, acc = s
        return x + 0.5 * acc

    return ([norm] + [chunk(c0, min(c0 + FF_CHUNK, ff)) for c0 in range(0, ff, FF_CHUNK)]
            + [residual])


def _row_blocks(tm):
    return list(range(0, tm, ROW_BLOCK))


def _rows(r):
    return slice(r, r + ROW_BLOCK)


def _ffn_proj_body(x_ref, g_ref, wg_ref, wu_ref, wd_ref, gm_ref, w_ref, y_ref, qkv_ref, *u_refs,
                   n_q):
    n, n_bf16 = w_ref.shape[1], qkv_ref.shape[1]

    def load(r, _):
        return x_ref[_rows(r), :]

    def mix_norm(r, y):
        y_ref[_rows(r), :] = y
        return _rmsnorm(y, gm_ref[...]).astype(BF16)

    def project(c0, c1):
        def stage(r, h):
            res = _dot(h, w_ref[:, c0:c1])
            lo, hi = c0, min(c1, n_q)
            if lo < hi:
                qkv_ref[_rows(r), lo:hi] = (res[:, lo - c0:hi - c0] * Q_SCALE).astype(BF16)
            lo, hi = max(c0, n_q), min(c1, n_bf16)
            if lo < hi:
                qkv_ref[_rows(r), lo:hi] = res[:, lo - c0:hi - c0].astype(BF16)
            lo, hi = max(c0, n_bf16), c1
            if lo < hi:
                u_refs[0][_rows(r), lo - n_bf16:hi - n_bf16] = res[:, lo - c0:hi - c0]
            return h
        return stage

    stages = ([load] + _ffn_stages(g_ref, wg_ref, wu_ref, wd_ref) + [mix_norm]
              + [project(c0, min(c0 + PROJ_CHUNK, n)) for c0 in range(0, n, PROJ_CHUNK)])
    _software_pipeline(_row_blocks(x_ref.shape[0]), stages, 1)


def _ffn_proj(x2, ffn, gm, w, layer, i, n_q, n_bf16):
    m, d = x2.shape
    n = w.shape[2]
    tm = TOKEN_TILE
    row_block = lambda width: pl.BlockSpec((tm, width), lambda t: (t, 0))
    out_specs = [row_block(d), row_block(n_bf16)]
    out_shape = [jax.ShapeDtypeStruct((m, d), F32), jax.ShapeDtypeStruct((m, n_bf16), BF16)]
    if n > n_bf16:
        out_specs.append(row_block(n - n_bf16))
        out_shape.append(jax.ShapeDtypeStruct((m, n - n_bf16), F32))
    return pl.pallas_call(
        functools.partial(_ffn_proj_body, n_q=n_q),
        grid=(m // tm,),
        in_specs=[row_block(d)] + _ffn_weight_specs(*ffn, layer) + [_layer(gm, layer), _layer(w, i)],
        out_specs=out_specs,
        out_shape=out_shape,
        compiler_params=pltpu.CompilerParams(
            dimension_semantics=("arbitrary",), vmem_limit_bytes=VMEM_LIMIT),
        name="ffn_proj",
    )(x2, *ffn, gm, w)


def _head_mask(hh):
    lane = lax.broadcasted_iota(jnp.int32, (1, LANES), 1)
    return (lane // HEAD_DIM) == hh


def _load_q(q_ref, r0, hmask, other_lanes):
    q = q_ref[0, pl.ds(r0, ATT_TILE), :]
    return jnp.where(hmask, q, other_lanes).astype(F32).T.astype(BF16)


def _stage_v_transposed(v_ref, vt_ref, nb):
    seq = vt_ref.shape[2]
    ones_row = lax.broadcasted_iota(jnp.int32, (ONES_ROWS, seq), 0) == 0
    for h in range(HEADS_PER_BLOCK):
        vt_ref[h, HEAD_DIM:, :] = jnp.where(ones_row, 1.0, 0.0).astype(BF16)
    for j in range(nb):
        cols = slice(j * ATT_TILE, (j + 1) * ATT_TILE)
        vt = v_ref[0, cols, :].astype(F32).T.astype(BF16)
        for h in range(HEADS_PER_BLOCK):
            vt_ref[h, :HEAD_DIM, cols] = vt[h * HEAD_DIM:(h + 1) * HEAD_DIM, :]


def _emit_output(ot_ref, o_ref, nb):
    for j in range(nb):
        cols = slice(j * ATT_TILE, (j + 1) * ATT_TILE)
        o_ref[0, cols, :] = ot_ref[:, cols].T.astype(o_ref.dtype)


def _software_pipeline(items, stages, width):
    groups = [items[i:i + width] for i in range(0, len(items), width)]
    state = {}
    for step in range(len(groups) + len(stages) - 1):
        for s, stage in enumerate(stages):
            x = step - s
            if 0 <= x < len(groups):
                for item in groups[x]:
                    state[item] = stage(item, state.get(item))


def _attn_call(body, qkv, n_heads, extra_inputs, extra_specs, extra_scratch, name):
    b, s, _ = qkv.shape
    nblk = n_heads // HEADS_PER_BLOCK
    nb = s // ATT_TILE
    blk = (1, s, LANES)
    return pl.pallas_call(
        functools.partial(body, nb=nb),
        grid=(b, nblk),
        in_specs=extra_specs + [
            pl.BlockSpec(blk, lambda b_, p: (b_, 0, p)),
            pl.BlockSpec(blk, lambda b_, p: (b_, 0, nblk + p)),
            pl.BlockSpec(blk, lambda b_, p: (b_, 0, 2 * nblk + p)),
        ],
        out_specs=pl.BlockSpec(blk, lambda b_, p: (b_, 0, p)),
        out_shape=jax.ShapeDtypeStruct((b, s, nblk * LANES), BF16),
        scratch_shapes=[pltpu.VMEM((HEADS_PER_BLOCK, HEAD_DIM + ONES_ROWS, s), BF16),
                        pltpu.VMEM((LANES, s), F32)] + extra_scratch,
        compiler_params=pltpu.CompilerParams(
            dimension_semantics=("arbitrary", "arbitrary"), vmem_limit_bytes=VMEM_LIMIT),
        name=name,
    )(*extra_inputs, qkv, qkv, qkv)


def _sb_body(q_ref, k_ref, v_ref, o_ref, vt_ref, ot_ref, *, nb):
    t = ATT_TILE
    _stage_v_transposed(v_ref, vt_ref, nb)
    s_r = lax.broadcasted_iota(jnp.int32, (t, t), 0)
    t_r = lax.broadcasted_iota(jnp.int32, (t, t), 1)
    past = s_r < t_r
    tri = (t_r > s_r).astype(BF16)
    no_lanes = jnp.zeros((1, LANES), BF16)
    qhs = {(hh, qb): _load_q(q_ref, qb * t, _head_mask(hh), no_lanes)
           for hh in range(HEADS_PER_BLOCK) for qb in range(nb)}
    parts = {}

    def scores(tile, _):
        hh, qb, j = tile
        return _dot(k_ref[0, j * t:(j + 1) * t, :], qhs[hh, qb])

    def log_terms(tile, z):
        hh, qb, j = tile
        zl = jnp.minimum(z, 0.0) - jnp.log2(1.0 + jnp.exp2(-jnp.abs(z)))
        lg = zl - z
        if j == qb:
            lg = jnp.where(past, lg, 0.0)
        return zl, lg[0:1, :], lg.astype(BF16)

    def later_sum(tile, s):
        zl, lg0, lgb = s
        return zl, lg0, _dot(tri, lgb)

    def weights(tile, s):
        hh, qb, j = tile
        zl, lg0, later = s
        a = jnp.exp2(zl + later)
        if j == qb:
            a = jnp.where(past, a, 0.0)
        return later[0:1, :] + lg0, a.astype(BF16)

    def values(tile, s):
        hh, qb, j = tile
        total, a = s
        pv = _dot(vt_ref[hh, :HEAD_DIM, j * t:(j + 1) * t], a)
        parts.setdefault((hh, qb), []).append((total, pv))
        if j == qb:
            run, acc = parts[hh, qb][-1]
            for total, pv in reversed(parts[hh, qb][:-1]):
                acc = acc + jnp.exp2(run) * pv
                run = run + total
            ot_ref[hh * HEAD_DIM:(hh + 1) * HEAD_DIM, qb * t:(qb + 1) * t] = acc

    tiles = [(hh, qb, j) for qb in range(nb) for j in range(qb + 1)
             for hh in range(HEADS_PER_BLOCK)]
    _software_pipeline(
        tiles,
        [lambda tile, s: log_terms(tile, scores(tile, s)),
         lambda tile, s: weights(tile, later_sum(tile, s)),
         values],
        HEADS_PER_BLOCK)
    _emit_output(ot_ref, o_ref, nb)


def _sb_attention(qkv, n_heads):
    return _attn_call(_sb_body, qkv, n_heads, [], [], [], "sb_attn")


def _moba_body(slopes_ref, q_ref, k_ref, v_ref, o_ref, vt_ref, ot_ref, ka_ref, *, nb):
    t = ATT_TILE
    pair = pl.program_id(1)
    _stage_v_transposed(v_ref, vt_ref, nb)
    causal = (lax.broadcasted_iota(jnp.int32, (t, t), 0)
              <= lax.broadcasted_iota(jnp.int32, (t, t), 1))
    blk = lax.broadcasted_iota(jnp.int32, (nb, t), 0)
    lane = lax.broadcasted_iota(jnp.int32, (1, LANES), 1)
    key_offset = lax.broadcasted_iota(jnp.int32, (t, LANES), 0).astype(F32).astype(BF16)

    kmean = jnp.concatenate(
        [jnp.sum(k_ref[0, j * t:(j + 1) * t, :].astype(F32), axis=0, keepdims=True)
         for j in range(nb)], axis=0) * (1.0 / t)

    slopes, kms, qhs = {}, {}, {}
    for hh in range(HEADS_PER_BLOCK):
        hmask = _head_mask(hh)
        slope = slopes_ref[pair * HEADS_PER_BLOCK + hh] * LOG2E
        spare = (1 - hh) * HEAD_DIM
        s0 = jnp.full((1, LANES), slope, F32)
        s1 = s0.astype(BF16).astype(F32)
        s2 = (s0 - s1).astype(BF16).astype(F32)
        s3 = s0 - s1 - s2
        slope_lanes = jnp.where(lane == spare, s1, jnp.where(
            lane == spare + 1, s2, jnp.where(lane == spare + 2, s3, 0.0))).astype(BF16)
        offset_lanes = (lane >= spare) & (lane < spare + 3)
        for j in range(nb):
            blk_rows = slice(j * t, (j + 1) * t)
            ka_ref[hh, blk_rows, :] = jnp.where(offset_lanes, key_offset, k_ref[0, blk_rows, :])
        slopes[hh] = slope
        kms[hh] = _split_bf16(jnp.where(hmask, kmean, 0.0))
        for qb in range(nb):
            qhs[hh, qb] = _load_q(q_ref, qb * t, hmask, slope_lanes)
    parts, unused = {}, {}

    def scores(tile, _):
        hh, qb, j = tile
        st = _dot(ka_ref[hh, j * t:(j + 1) * t, :], qhs[hh, qb])
        return jnp.where(causal, st, MASKED) if j == qb else st

    def softmax(tile, st):
        mx = jnp.max(st, axis=0, keepdims=True)
        return mx, jnp.exp2(st - mx).astype(BF16)

    def values(tile, s):
        hh, qb, j = tile
        mx, p = s
        pv = _dot(vt_ref[hh, :, j * t:(j + 1) * t], p)
        ms = mx - slopes[hh] * float((qb - j) * t)
        if qb > MOBA_TOPK:
            if j == 0:
                unused[hh, qb] = unused_blocks(hh, qb)
            if j < qb:
                ms = ms + unused[hh, qb][j:j + 1, :]
        parts.setdefault((hh, qb), []).append((ms, pv))
        if j == qb:
            merge(hh, qb, parts[hh, qb])

    def unused_blocks(hh, qb):
        km_hi, km_lo = kms[hh]
        gate = _dot(km_hi, qhs[hh, qb]) + _dot(km_lo, qhs[hh, qb])
        rank = jnp.zeros((nb, t), jnp.int32)
        for n in range(qb):
            other = gate[n:n + 1, :]
            beats = (other > gate) | ((other == gate) & (n < blk))
            rank = rank + jnp.where(beats, 1, 0)
        return jnp.where(rank < MOBA_TOPK, 0.0, MASKED)

    def merge(hh, qb, tile_parts):
        m = tile_parts[0][0]
        for ms, _ in tile_parts[1:]:
            m = jnp.maximum(m, ms)
        acc = None
        for ms, pv in tile_parts:
            w = jnp.exp2(ms - m)
            acc = w * pv if acc is None else acc + w * pv
        ot_ref[hh * HEAD_DIM:(hh + 1) * HEAD_DIM, qb * t:(qb + 1) * t] = (
            acc[:HEAD_DIM, :] / acc[HEAD_DIM:HEAD_DIM + 1, :])

    tiles = [(hh, qb, j) for qb in range(nb) for j in range(qb + 1)
             for hh in range(HEADS_PER_BLOCK)]
    _software_pipeline(tiles, [scores, softmax, values], HEADS_PER_BLOCK)
    _emit_output(ot_ref, o_ref, nb)


def _moba_attention(qkv, n_heads):
    slopes = jnp.asarray(2.0 ** (-8.0 * np.arange(1, n_heads + 1) / n_heads), dtype=F32)
    return _attn_call(_moba_body, qkv, n_heads, [slopes], [pl.BlockSpec(memory_space=pltpu.SMEM)],
                      [pltpu.VMEM((HEADS_PER_BLOCK, qkv.shape[1], LANES), BF16)], "moba_attn")


def _out_ffn_pipeline(mix, ffn_refs, gf_ref, store, tm, final):
    def finish(r, z):
        store(r, _rmsnorm(z, gf_ref[...]) if final else z)

    _software_pipeline(_row_blocks(tm), [mix] + _ffn_stages(*ffn_refs) + [finish], 1)


def _even_out_ffn_body(a_ref, u_ref, uprev_ref, wpool_ref, pscale_ref, wout_ref, x_ref,
                       g_ref, wg_ref, wu_ref, wd_ref, gf_ref, o_ref, ucat_ref, *, tm, final):
    n_attn = a_ref.shape[2]
    ti = pl.program_id(1)
    halo = uprev_ref[0]
    ucat_ref[0:POOL_HALO, :] = jnp.where(ti > 0, halo, jnp.zeros_like(halo))
    ucat_ref[POOL_HALO:, :] = u_ref[0]

    def mix(r, _):
        rows = _rows(r)
        pos = ti * tm + r + lax.broadcasted_iota(jnp.int32, (ROW_BLOCK, 1), 0)
        mixed = []
        for g, w in enumerate(POOL_WINDOWS):
            lanes = slice(g * LANES, (g + 1) * LANES)
            first = POOL_HALO + r
            win = ucat_ref[first:first + ROW_BLOCK, lanes]
            for i in range(1, w):
                win = win + ucat_ref[first - i:first - i + ROW_BLOCK, lanes]
            count = jnp.minimum(pos + 1, w).astype(F32)
            pooled = win / count - ucat_ref[first:first + ROW_BLOCK, lanes]
            mixed.append(_dot(pooled.astype(BF16), wpool_ref[g]) * pscale_ref[:, lanes])
        p = jnp.concatenate(mixed, axis=1).astype(BF16)
        return (x_ref[0, rows, :] + _dot(a_ref[0, rows, :], wout_ref[:n_attn, :])
                + _dot(p, wout_ref[n_attn:, :]))

    def store(r, z):
        o_ref[0, _rows(r), :] = z

    _out_ffn_pipeline(mix, (g_ref, wg_ref, wu_ref, wd_ref), gf_ref, store, tm, final)


def _even_out_ffn(attn, u, wpool, pscale, wout, i, x, ffn, gf, layer, final):
    b, s, d = x.shape
    n_attn = attn.shape[2]
    n_pool = u.shape[2]
    tm = OUT_TOKEN_TILE
    halo_blocks = tm // POOL_HALO
    return pl.pallas_call(
        functools.partial(_even_out_ffn_body, tm=tm, final=final),
        grid=(b, s // tm),
        in_specs=[
            pl.BlockSpec((1, tm, n_attn), lambda b_, t: (b_, t, 0)),
            pl.BlockSpec((1, tm, n_pool), lambda b_, t: (b_, t, 0)),
            pl.BlockSpec((1, POOL_HALO, n_pool),
                         lambda b_, t: (b_, jnp.maximum(t * halo_blocks - 1, 0), 0)),
            pl.BlockSpec((None,) + wpool.shape[1:], lambda *_: (i, 0, 0, 0),
                         pipeline_mode=pl.Buffered(1)),
            _layer(pscale, i),
            _layer(wout, i),
            pl.BlockSpec((1, tm, d), lambda b_, t: (b_, t, 0)),
        ] + _ffn_weight_specs(*ffn, layer) + [_resident((1, d))],
        out_specs=pl.BlockSpec((1, tm, d), lambda b_, t: (b_, t, 0)),
        out_shape=jax.ShapeDtypeStruct((b, s, d), F32),
        scratch_shapes=[pltpu.VMEM((POOL_HALO + tm, n_pool), F32)],
        compiler_params=pltpu.CompilerParams(
            dimension_semantics=("arbitrary", "arbitrary"), vmem_limit_bytes=VMEM_LIMIT),
        name="even_out_ffn",
    )(attn, u, u, wpool, pscale, wout, x, *ffn, gf)


def _odd_out_ffn_body(o_ref, w_ref, x_ref, g_ref, wg_ref, wu_ref, wd_ref, gf_ref, y_ref, *, final):
    def mix(r, _):
        return x_ref[_rows(r), :] + _dot(o_ref[_rows(r), :], w_ref[...])

    def store(r, z):
        y_ref[_rows(r), :] = z

    _out_ffn_pipeline(mix, (g_ref, wg_ref, wu_ref, wd_ref), gf_ref, store, x_ref.shape[0], final)


def _odd_out_ffn(o2, w, i, x2, ffn, gf, layer, final):
    m, d = x2.shape
    tm = OUT_TOKEN_TILE
    return pl.pallas_call(
        functools.partial(_odd_out_ffn_body, final=final),
        grid=(m // tm,),
        in_specs=[pl.BlockSpec((tm, o2.shape[1]), lambda t: (t, 0)), _layer(w, i),
                  pl.BlockSpec((tm, d), lambda t: (t, 0))]
        + _ffn_weight_specs(*ffn, layer) + [_resident((1, d))],
        out_specs=pl.BlockSpec((tm, d), lambda t: (t, 0)),
        out_shape=jax.ShapeDtypeStruct((m, d), F32),
        compiler_params=pltpu.CompilerParams(
            dimension_semantics=("arbitrary",), vmem_limit_bytes=VMEM_LIMIT),
        name="odd_out_ffn",
    )(o2, w, x2, *ffn, gf)


def kernel(x, norm_ffn1, ffn1_gate, ffn1_up, ffn1_down, norm_mix, norm_ffn2, ffn2_gate, ffn2_up,
           ffn2_down, even_w_in, even_w_pool, even_pool_scale, even_w_out, odd_w_qkv, odd_w_o,
           norm_final):
    b, s, d = x.shape
    depth = norm_ffn1.shape[0]
    m = b * s
    sb_width = even_w_in.shape[2] - even_w_pool.shape[1] * even_w_pool.shape[2]
    sb_heads = sb_width // (3 * HEAD_DIM)
    moba_heads = odd_w_qkv.shape[2] // (3 * HEAD_DIM)
    gf = norm_final.reshape(1, d)
    g1, gm, g2 = (g.reshape(depth, 1, d) for g in (norm_ffn1, norm_mix, norm_ffn2))
    ffn1 = [g1] + [w.astype(BF16) for w in (ffn1_gate, ffn1_up, ffn1_down)]
    ffn2 = [g2] + [w.astype(BF16) for w in (ffn2_gate, ffn2_up, ffn2_down)]
    w_in, w_pool, w_out = (w.astype(BF16) for w in (even_w_in, even_w_pool, even_w_out))
    w_qkv, w_o = odd_w_qkv.astype(BF16), odd_w_o.astype(BF16)
    pool_scale = even_pool_scale.reshape(even_pool_scale.shape[0], 1, -1)

    x2 = x.reshape(m, d)
    for layer in range(depth):
        i = layer // 2
        final = layer == depth - 1
        if layer % 2 == 0:
            x2, qkv, u = _ffn_proj(x2, ffn1, gm, w_in, layer, i, sb_width // 3, sb_width)
            attn = _sb_attention(qkv.reshape(b, s, sb_width), sb_heads)
            x2 = _even_out_ffn(attn, u.reshape(b, s, -1), w_pool, pool_scale, w_out, i,
                               x2.reshape(b, s, d), ffn2, gf, layer, final).reshape(m, d)
        else:
            n_qkv = w_qkv.shape[2]
            x2, qkv = _ffn_proj(x2, ffn1, gm, w_qkv, layer, i, n_qkv // 3, n_qkv)
            o = _moba_attention(qkv.reshape(b, s, n_qkv), moba_heads)
            x2 = _odd_out_ffn(o.reshape(m, -1), w_o, i, x2, ffn2, gf, layer, final)
    return x2.reshape(b, s, d)
```

```python
import functools

import jax
import jax.numpy as jnp
import numpy as np
from jax import lax
from jax.experimental import pallas as pl
from jax.experimental.pallas import tpu as pltpu

F32 = jnp.float32
BF16 = jnp.bfloat16

LANES = 128
HEAD_DIM = 64
HEADS_PER_BLOCK = LANES // HEAD_DIM
ATT_TILE = 256
MOBA_TOPK = 3
POOL_WINDOWS = (2, 4, 8, 16)
POOL_HALO = 16
RMS_EPS = 1e-6
MASKED = -1e30
LOG2E = 1.4426950408889634
Q_SCALE = LOG2E * HEAD_DIM ** -0.5
ONES_ROWS = 16
TOKEN_TILE = 512
OUT_TOKEN_TILE = 1024
FF_CHUNK = 1024
PROJ_CHUNK = 1024
ROW_BLOCK = 256
VMEM_LIMIT = 56 * 1024 * 1024


def _rmsnorm(x, g):
    return x * lax.rsqrt(jnp.mean(x * x, axis=-1, keepdims=True) + RMS_EPS) * g


def _resident(shape):
    nd = len(shape)
    return pl.BlockSpec(shape, lambda *_: (0,) * nd, pipeline_mode=pl.Buffered(1))


def _layer(stack, layer):
    _, r, c = stack.shape
    return pl.BlockSpec((None, r, c), lambda *_: (layer, 0, 0), pipeline_mode=pl.Buffered(1))


def _dot(a, b):
    return jnp.dot(a, b, preferred_element_type=F32)


def _split_bf16(x):
    hi = x.astype(BF16)
    lo = (x - hi.astype(F32)).astype(BF16)
    return hi, lo


def _ffn_weight_specs(g, wg, wu, wd, layer):
    return [_layer(g, layer), _layer(wg, layer), _layer(wu, layer), _layer(wd, layer)]


def _ffn_stages(g_ref, wg_ref, wu_ref, wd_ref):
    ff = wg_ref.shape[1]

    def norm(r, x):
        return x, _rmsnorm(x, g_ref[...]).astype(BF16), None

    def chunk(c0, c1):
        def stage(r, s):
            x, h, acc = s
            gate = _dot(h, wg_ref[:, c0:c1])
            up = _dot(h, wu_ref[:, c0:c1])
            act = (gate / (1.0 + jnp.exp(-gate)) * up).astype(BF16)
            down = _dot(act, wd_ref[c0:c1, :])
            return x, h, (down if acc is None else acc + down)
        return stage

    def residual(r, s):
        x, _, acc = s
        return x + 0.5 * acc

    return ([norm] + [chunk(c0, min(c0 + FF_CHUNK, ff)) for c0 in range(0, ff, FF_CHUNK)]
            + [residual])


def _row_blocks(tm):
    return list(range(0, tm, ROW_BLOCK))


def _rows(r):
    return slice(r, r + ROW_BLOCK)


def _ffn_proj_body(x_ref, g_ref, wg_ref, wu_ref, wd_ref, gm_ref, w_ref, y_ref, qkv_ref, *u_refs,
                   n_q):
    n, n_bf16 = w_ref.shape[1], qkv_ref.shape[1]

    def load(r, _):
        return x_ref[_rows(r), :]

    def mix_norm(r, y):
        y_ref[_rows(r), :] = y
        return _rmsnorm(y, gm_ref[...]).astype(BF16)

    def project(c0, c1):
        def stage(r, h):
            res = _dot(h, w_ref[:, c0:c1])
            lo, hi = c0, min(c1, n_q)
            if lo < hi:
                qkv_ref[_rows(r), lo:hi] = (res[:, lo - c0:hi - c0] * Q_SCALE).astype(BF16)
            lo, hi = max(c0, n_q), min(c1, n_bf16)
            if lo < hi:
                qkv_ref[_rows(r), lo:hi] = res[:, lo - c0:hi - c0].astype(BF16)
            lo, hi = max(c0, n_bf16), c1
            if lo < hi:
                u_refs[0][_rows(r), lo - n_bf16:hi - n_bf16] = res[:, lo - c0:hi - c0]
            return h
        return stage

    stages = ([load] + _ffn_stages(g_ref, wg_ref, wu_ref, wd_ref) + [mix_norm]
              + [project(c0, min(c0 + PROJ_CHUNK, n)) for c0 in range(0, n, PROJ_CHUNK)])
    _software_pipeline(_row_blocks(x_ref.shape[0]), stages, 1, oldest_first=True)


def _ffn_proj(x2, ffn, gm, w, layer, i, n_q, n_bf16):
    m, d = x2.shape
    n = w.shape[2]
    tm = TOKEN_TILE
    row_block = lambda width: pl.BlockSpec((tm, width), lambda t: (t, 0))
    out_specs = [row_block(d), row_block(n_bf16)]
    out_shape = [jax.ShapeDtypeStruct((m, d), F32), jax.ShapeDtypeStruct((m, n_bf16), BF16)]
    if n > n_bf16:
        out_specs.append(row_block(n - n_bf16))
        out_shape.append(jax.ShapeDtypeStruct((m, n - n_bf16), F32))
    return pl.pallas_call(
        functools.partial(_ffn_proj_body, n_q=n_q),
        grid=(m // tm,),
        in_specs=[row_block(d)] + _ffn_weight_specs(*ffn, layer) + [_layer(gm, layer), _layer(w, i)],
        out_specs=out_specs,
        out_shape=out_shape,
        compiler_params=pltpu.CompilerParams(
            dimension_semantics=("arbitrary",), vmem_limit_bytes=VMEM_LIMIT),
        name="ffn_proj",
    )(x2, *ffn, gm, w)


def _head_mask(hh):
    lane = lax.broadcasted_iota(jnp.int32, (1, LANES), 1)
    return (lane // HEAD_DIM) == hh


def _load_q(q_ref, r0, hmask, other_lanes):
    q = q_ref[0, pl.ds(r0, ATT_TILE), :]
    return jnp.where(hmask, q, other_lanes).astype(F32).T.astype(BF16)


def _stage_v_transposed(v_ref, vt_ref, nb):
    seq = vt_ref.shape[2]
    ones_row = lax.broadcasted_iota(jnp.int32, (ONES_ROWS, seq), 0) == 0
    for h in range(HEADS_PER_BLOCK):
        vt_ref[h, HEAD_DIM:, :] = jnp.where(ones_row, 1.0, 0.0).astype(BF16)
    for j in range(nb):
        cols = slice(j * ATT_TILE, (j + 1) * ATT_TILE)
        vt = v_ref[0, cols, :].astype(F32).T.astype(BF16)
        for h in range(HEADS_PER_BLOCK):
            vt_ref[h, :HEAD_DIM, cols] = vt[h * HEAD_DIM:(h + 1) * HEAD_DIM, :]


def _emit_output(ot_ref, o_ref, nb):
    for j in range(nb):
        cols = slice(j * ATT_TILE, (j + 1) * ATT_TILE)
        o_ref[0, cols, :] = ot_ref[:, cols].T.astype(o_ref.dtype)


def _software_pipeline(items, stages, width, oldest_first=False):
    groups = [items[i:i + width] for i in range(0, len(items), width)]
    state = {}
    order = list(enumerate(stages))
    if oldest_first:
        order.reverse()
    for step in range(len(groups) + len(stages) - 1):
        for s, stage in order:
            x = step - s
            if 0 <= x < len(groups):
                for item in groups[x]:
                    state[item] = stage(item, state.get(item))


def _attn_call(body, qkv, n_heads, extra_inputs, extra_specs, extra_scratch, name):
    b, s, _ = qkv.shape
    nblk = n_heads // HEADS_PER_BLOCK
    nb = s // ATT_TILE
    blk = (1, s, LANES)
    return pl.pallas_call(
        functools.partial(body, nb=nb),
        grid=(b, nblk),
        in_specs=extra_specs + [
            pl.BlockSpec(blk, lambda b_, p: (b_, 0, p)),
            pl.BlockSpec(blk, lambda b_, p: (b_, 0, nblk + p)),
            pl.BlockSpec(blk, lambda b_, p: (b_, 0, 2 * nblk + p)),
        ],
        out_specs=pl.BlockSpec(blk, lambda b_, p: (b_, 0, p)),
        out_shape=jax.ShapeDtypeStruct((b, s, nblk * LANES), BF16),
        scratch_shapes=[pltpu.VMEM((HEADS_PER_BLOCK, HEAD_DIM + ONES_ROWS, s), BF16),
                        pltpu.VMEM((LANES, s), F32)] + extra_scratch,
        compiler_params=pltpu.CompilerParams(
            dimension_semantics=("arbitrary", "arbitrary"), vmem_limit_bytes=VMEM_LIMIT),
        name=name,
    )(*extra_inputs, qkv, qkv, qkv)


def _sb_body(q_ref, k_ref, v_ref, o_ref, vt_ref, ot_ref, *, nb):
    t = ATT_TILE
    _stage_v_transposed(v_ref, vt_ref, nb)
    s_r = lax.broadcasted_iota(jnp.int32, (t, t), 0)
    t_r = lax.broadcasted_iota(jnp.int32, (t, t), 1)
    past = s_r < t_r
    tri = (t_r > s_r).astype(BF16)
    no_lanes = jnp.zeros((1, LANES), BF16)
    qhs = {(hh, qb): _load_q(q_ref, qb * t, _head_mask(hh), no_lanes)
           for hh in range(HEADS_PER_BLOCK) for qb in range(nb)}
    parts = {}

    def scores(tile, _):
        hh, qb, j = tile
        return _dot(k_ref[0, j * t:(j + 1) * t, :], qhs[hh, qb])

    def log_terms(tile, z):
        hh, qb, j = tile
        zl = jnp.minimum(z, 0.0) - jnp.log2(1.0 + jnp.exp2(-jnp.abs(z)))
        lg = zl - z
        if j == qb:
            lg = jnp.where(past, lg, 0.0)
        return zl, lg[0:1, :], lg.astype(BF16)

    def later_sum(tile, s):
        zl, lg0, lgb = s
        return zl, lg0, _dot(tri, lgb)

    def weights(tile, s):
        hh, qb, j = tile
        zl, lg0, later = s
        a = jnp.exp2(zl + later)
        if j == qb:
            a = jnp.where(past, a, 0.0)
        return later[0:1, :] + lg0, a.astype(BF16)

    def values(tile, s):
        hh, qb, j = tile
        total, a = s
        pv = _dot(vt_ref[hh, :HEAD_DIM, j * t:(j + 1) * t], a)
        parts.setdefault((hh, qb), []).append((total, pv))
        if j == qb:
            run, acc = parts[hh, qb][-1]
            for total, pv in reversed(parts[hh, qb][:-1]):
                acc = acc + jnp.exp2(run) * pv
                run = run + total
            ot_ref[hh * HEAD_DIM:(hh + 1) * HEAD_DIM, qb * t:(qb + 1) * t] = acc

    tiles = [(hh, qb, j) for qb in range(nb) for j in range(qb + 1)
             for hh in range(HEADS_PER_BLOCK)]
    _software_pipeline(
        tiles,
        [lambda tile, s: log_terms(tile, scores(tile, s)),
         lambda tile, s: weights(tile, later_sum(tile, s)),
         values],
        HEADS_PER_BLOCK)
    _emit_output(ot_ref, o_ref, nb)


def _sb_attention(qkv, n_heads):
    return _attn_call(_sb_body, qkv, n_heads, [], [], [], "sb_attn")


def _moba_body(slopes_ref, q_ref, k_ref, v_ref, o_ref, vt_ref, ot_ref, ka_ref, *, nb):
    t = ATT_TILE
    pair = pl.program_id(1)
    _stage_v_transposed(v_ref, vt_ref, nb)
    causal = (lax.broadcasted_iota(jnp.int32, (t, t), 0)
              <= lax.broadcasted_iota(jnp.int32, (t, t), 1))
    blk = lax.broadcasted_iota(jnp.int32, (nb, t), 0)
    lane = lax.broadcasted_iota(jnp.int32, (1, LANES), 1)
    key_offset = lax.broadcasted_iota(jnp.int32, (t, LANES), 0).astype(F32).astype(BF16)

    kmean = jnp.concatenate(
        [jnp.sum(k_ref[0, j * t:(j + 1) * t, :].astype(F32), axis=0, keepdims=True)
         for j in range(nb)], axis=0) * (1.0 / t)

    slopes, kms, qhs = {}, {}, {}
    for hh in range(HEADS_PER_BLOCK):
        hmask = _head_mask(hh)
        slope = slopes_ref[pair * HEADS_PER_BLOCK + hh] * LOG2E
        spare = (1 - hh) * HEAD_DIM
        s0 = jnp.full((1, LANES), slope, F32)
        s1 = s0.astype(BF16).astype(F32)
        s2 = (s0 - s1).astype(BF16).astype(F32)
        s3 = s0 - s1 - s2
        slope_lanes = jnp.where(lane == spare, s1, jnp.where(
            lane == spare + 1, s2, jnp.where(lane == spare + 2, s3, 0.0))).astype(BF16)
        offset_lanes = (lane >= spare) & (lane < spare + 3)
        for j in range(nb):
            blk_rows = slice(j * t, (j + 1) * t)
            ka_ref[hh, blk_rows, :] = jnp.where(offset_lanes, key_offset, k_ref[0, blk_rows, :])
        slopes[hh] = slope
        kms[hh] = _split_bf16(jnp.where(hmask, kmean, 0.0))
        for qb in range(nb):
            qhs[hh, qb] = _load_q(q_ref, qb * t, hmask, slope_lanes)
    parts, unused = {}, {}

    def scores(tile, _):
        hh, qb, j = tile
        st = _dot(ka_ref[hh, j * t:(j + 1) * t, :], qhs[hh, qb])
        return jnp.where(causal, st, MASKED) if j == qb else st

    def softmax(tile, st):
        mx = jnp.max(st, axis=0, keepdims=True)
        return mx, jnp.exp2(st - mx).astype(BF16)

    def values(tile, s):
        hh, qb, j = tile
        mx, p = s
        pv = _dot(vt_ref[hh, :, j * t:(j + 1) * t], p)
        ms = mx - slopes[hh] * float((qb - j) * t)
        if qb > MOBA_TOPK:
            if j == 0:
                unused[hh, qb] = unused_blocks(hh, qb)
            if j < qb:
                ms = ms + unused[hh, qb][j:j + 1, :]
        parts.setdefault((hh, qb), []).append((ms, pv))
        if j == qb:
            merge(hh, qb, parts[hh, qb])

    def unused_blocks(hh, qb):
        km_hi, km_lo = kms[hh]
        gate = _dot(km_hi, qhs[hh, qb]) + _dot(km_lo, qhs[hh, qb])
        rank = jnp.zeros((nb, t), jnp.int32)
        for n in range(qb):
            other = gate[n:n + 1, :]
            beats = (other > gate) | ((other == gate) & (n < blk))
            rank = rank + jnp.where(beats, 1, 0)
        return jnp.where(rank < MOBA_TOPK, 0.0, MASKED)

    def merge(hh, qb, tile_parts):
        m = tile_parts[0][0]
        for ms, _ in tile_parts[1:]:
            m = jnp.maximum(m, ms)
        acc = None
        for ms, pv in tile_parts:
            w = jnp.exp2(ms - m)
            acc = w * pv if acc is None else acc + w * pv
        ot_ref[hh * HEAD_DIM:(hh + 1) * HEAD_DIM, qb * t:(qb + 1) * t] = (
            acc[:HEAD_DIM, :] / acc[HEAD_DIM:HEAD_DIM + 1, :])

    tiles = [(hh, qb, j) for qb in range(nb) for j in range(qb + 1)
             for hh in range(HEADS_PER_BLOCK)]
    _software_pipeline(tiles, [scores, softmax, values], HEADS_PER_BLOCK)
    _emit_output(ot_ref, o_ref, nb)


def _moba_attention(qkv, n_heads):
    slopes = jnp.asarray(2.0 ** (-8.0 * np.arange(1, n_heads + 1) / n_heads), dtype=F32)
    return _attn_call(_moba_body, qkv, n_heads, [slopes], [pl.BlockSpec(memory_space=pltpu.SMEM)],
                      [pltpu.VMEM((HEADS_PER_BLOCK, qkv.shape[1], LANES), BF16)], "moba_attn")


def _out_ffn_pipeline(mix, ffn_refs, gf_ref, store, tm, final):
    def finish(r, z):
        store(r, _rmsnorm(z, gf_ref[...]) if final else z)

    _software_pipeline(_row_blocks(tm), [mix] + _ffn_stages(*ffn_refs) + [finish], 1,
                       oldest_first=True)


def _even_out_ffn_body(a_ref, u_ref, uprev_ref, wpool_ref, pscale_ref, wout_ref, x_ref,
                       g_ref, wg_ref, wu_ref, wd_ref, gf_ref, o_ref, ucat_ref, *, tm, final):
    n_attn = a_ref.shape[2]
    ti = pl.program_id(1)
    halo = uprev_ref[0]
    ucat_ref[0:POOL_HALO, :] = jnp.where(ti > 0, halo, jnp.zeros_like(halo))
    ucat_ref[POOL_HALO:, :] = u_ref[0]

    def mix(r, _):
        rows = _rows(r)
        pos = ti * tm + r + lax.broadcasted_iota(jnp.int32, (ROW_BLOCK, 1), 0)
        mixed = []
        for g, w in enumerate(POOL_WINDOWS):
            lanes = slice(g * LANES, (g + 1) * LANES)
            first = POOL_HALO + r
            win = ucat_ref[first:first + ROW_BLOCK, lanes]
            for i in range(1, w):
                win = win + ucat_ref[first - i:first - i + ROW_BLOCK, lanes]
            count = jnp.minimum(pos + 1, w).astype(F32)
            pooled = win / count - ucat_ref[first:first + ROW_BLOCK, lanes]
            mixed.append(_dot(pooled.astype(BF16), wpool_ref[g]) * pscale_ref[:, lanes])
        p = jnp.concatenate(mixed, axis=1).astype(BF16)
        return (x_ref[0, rows, :] + _dot(a_ref[0, rows, :], wout_ref[:n_attn, :])
                + _dot(p, wout_ref[n_attn:, :]))

    def store(r, z):
        o_ref[0, _rows(r), :] = z

    _out_ffn_pipeline(mix, (g_ref, wg_ref, wu_ref, wd_ref), gf_ref, store, tm, final)


def _even_out_ffn(attn, u, wpool, pscale, wout, i, x, ffn, gf, layer, final):
    b, s, d = x.shape
    n_attn = attn.shape[2]
    n_pool = u.shape[2]
    tm = OUT_TOKEN_TILE
    halo_blocks = tm // POOL_HALO
    return pl.pallas_call(
        functools.partial(_even_out_ffn_body, tm=tm, final=final),
        grid=(b, s // tm),
        in_specs=[
            pl.BlockSpec((1, tm, n_attn), lambda b_, t: (b_, t, 0)),
            pl.BlockSpec((1, tm, n_pool), lambda b_, t: (b_, t, 0)),
            pl.BlockSpec((1, POOL_HALO, n_pool),
                         lambda b_, t: (b_, jnp.maximum(t * halo_blocks - 1, 0), 0)),
            pl.BlockSpec((None,) + wpool.shape[1:], lambda *_: (i, 0, 0, 0),
                         pipeline_mode=pl.Buffered(1)),
            _layer(pscale, i),
            _layer(wout, i),
            pl.BlockSpec((1, tm, d), lambda b_, t: (b_, t, 0)),
        ] + _ffn_weight_specs(*ffn, layer) + [_resident((1, d))],
        out_specs=pl.BlockSpec((1, tm, d), lambda b_, t: (b_, t, 0)),
        out_shape=jax.ShapeDtypeStruct((b, s, d), F32),
        scratch_shapes=[pltpu.VMEM((POOL_HALO + tm, n_pool), F32)],
        compiler_params=pltpu.CompilerParams(
            dimension_semantics=("arbitrary", "arbitrary"), vmem_limit_bytes=VMEM_LIMIT),
        name="even_out_ffn",
    )(attn, u, u, wpool, pscale, wout, x, *ffn, gf)


def _odd_out_ffn_body(o_ref, w_ref, x_ref, g_ref, wg_ref, wu_ref, wd_ref, gf_ref, y_ref, *, final):
    def mix(r, _):
        return x_ref[_rows(r), :] + _dot(o_ref[_rows(r), :], w_ref[...])

    def store(r, z):
        y_ref[_rows(r), :] = z

    _out_ffn_pipeline(mix, (g_ref, wg_ref, wu_ref, wd_ref), gf_ref, store, x_ref.shape[0], final)


def _odd_out_ffn(o2, w, i, x2, ffn, gf, layer, final):
    m, d = x2.shape
    tm = OUT_TOKEN_TILE
    return pl.pallas_call(
        functools.partial(_odd_out_ffn_body, final=final),
        grid=(m // tm,),
        in_specs=[pl.BlockSpec((tm, o2.shape[1]), lambda t: (t, 0)), _layer(w, i),
                  pl.BlockSpec((tm, d), lambda t: (t, 0))]
        + _ffn_weight_specs(*ffn, layer) + [_resident((1, d))],
        out_specs=pl.BlockSpec((tm, d), lambda t: (t, 0)),
        out_shape=jax.ShapeDtypeStruct((m, d), F32),
        compiler_params=pltpu.CompilerParams(
            dimension_semantics=("arbitrary",), vmem_limit_bytes=VMEM_LIMIT),
        name="odd_out_ffn",
    )(o2, w, x2, *ffn, gf)


def kernel(x, norm_ffn1, ffn1_gate, ffn1_up, ffn1_down, norm_mix, norm_ffn2, ffn2_gate, ffn2_up,
           ffn2_down, even_w_in, even_w_pool, even_pool_scale, even_w_out, odd_w_qkv, odd_w_o,
           norm_final):
    b, s, d = x.shape
    depth = norm_ffn1.shape[0]
    m = b * s
    sb_width = even_w_in.shape[2] - even_w_pool.shape[1] * even_w_pool.shape[2]
    sb_heads = sb_width // (3 * HEAD_DIM)
    moba_heads = odd_w_qkv.shape[2] // (3 * HEAD_DIM)
    gf = norm_final.reshape(1, d)
    g1, gm, g2 = (g.reshape(depth, 1, d) for g in (norm_ffn1, norm_mix, norm_ffn2))
    ffn1 = [g1] + [w.astype(BF16) for w in (ffn1_gate, ffn1_up, ffn1_down)]
    ffn2 = [g2] + [w.astype(BF16) for w in (ffn2_gate, ffn2_up, ffn2_down)]
    w_in, w_pool, w_out = (w.astype(BF16) for w in (even_w_in, even_w_pool, even_w_out))
    w_qkv, w_o = odd_w_qkv.astype(BF16), odd_w_o.astype(BF16)
    pool_scale = even_pool_scale.reshape(even_pool_scale.shape[0], 1, -1)

    x2 = x.reshape(m, d)
    for layer in range(depth):
        i = layer // 2
        final = layer == depth - 1
        if layer % 2 == 0:
            x2, qkv, u = _ffn_proj(x2, ffn1, gm, w_in, layer, i, sb_width // 3, sb_width)
            attn = _sb_attention(qkv.reshape(b, s, sb_width), sb_heads)
            x2 = _even_out_ffn(attn, u.reshape(b, s, -1), w_pool, pool_scale, w_out, i,
                               x2.reshape(b, s, d), ffn2, gf, layer, final).reshape(m, d)
        else:
            n_qkv = w_qkv.shape[2]
            x2, qkv = _ffn_proj(x2, ffn1, gm, w_qkv, layer, i, n_qkv // 3, n_qkv)
            o = _moba_attention(qkv.reshape(b, s, n_qkv), moba_heads)
            x2 = _odd_out_ffn(o.reshape(m, -1), w_o, i, x2, ffn2, gf, layer, final)
    return x2.reshape(b, s, d)
```
